```python
import math
import jax, jax.numpy as jnp
from jax import lax
import numpy as np

D_MODEL = 2048
BATCH = 2
SEQ = 4096
DEPTH = 4

CHUNK = 64
EPS = 1e-6

SSM_EXPAND = 2
D_INNER = SSM_EXPAND * D_MODEL
SSM_HEAD_DIM = 64
SSM_HEADS = D_INNER // SSM_HEAD_DIM
SSM_GROUPS = 8
SSM_HPG = SSM_HEADS // SSM_GROUPS
SSM_STATE = 128
SSM_CONV = 4
CONV_DIM = D_INNER + 2 * SSM_GROUPS * SSM_STATE

ATTN_HEAD_DIM = 128
ATTN_Q_HEADS = 16
ATTN_KV_HEADS = 4
ATTN_QPK = ATTN_Q_HEADS // ATTN_KV_HEADS
ATTN_SCALE = ATTN_HEAD_DIM ** -0.5
IDX_HEADS = 16
IDX_HEAD_DIM = 64
IDX_SCALE = (IDX_HEAD_DIM ** -0.5) * (IDX_HEADS ** -0.5)
TOPK_MAX = 256
Q_BLOCK = 128

N_BRANCHES = 2

D_FF = 2 * D_MODEL
FFN_CONV = 3

IN_SPLITS = (D_INNER, CONV_DIM, SSM_HEADS,
             ATTN_Q_HEADS * ATTN_HEAD_DIM, ATTN_KV_HEADS * ATTN_HEAD_DIM, ATTN_KV_HEADS * ATTN_HEAD_DIM,
             IDX_HEADS * IDX_HEAD_DIM, IDX_HEAD_DIM, IDX_HEADS,
             N_BRANCHES * D_MODEL)
D_IN_PROJ = sum(IN_SPLITS)

kernel_name = "hybrid_ssd_dsa_gated_sandwich"


def rmsnorm(x, g):
    x32 = x.astype(jnp.float32)
    y = x32 * lax.rsqrt(jnp.mean(x32 * x32, axis=-1, keepdims=True) + EPS)
    return (y * g.astype(jnp.float32)).astype(x.dtype)


def causal_dwconv(x, w, b):
    k = w.shape[0]
    length = x.shape[1]
    xp = jnp.pad(x, ((0, 0), (k - 1, 0), (0, 0)))
    y = b
    for i in range(k):
        y = y + xp[:, i:i + length] * w[i]
    return y


def split_in_proj(proj):
    offsets = [int(o) for o in np.cumsum(IN_SPLITS)[:-1]]
    return jnp.split(proj, offsets, axis=-1)


def ssd_chunked(x, dt, a, bm, cm):
    b, s = x.shape[:2]
    c = s // CHUNK
    x = x.reshape(b, c, CHUNK, SSM_GROUPS, SSM_HPG, SSM_HEAD_DIM)
    dt = dt.reshape(b, c, CHUNK, SSM_GROUPS, SSM_HPG)
    bm = bm.reshape(b, c, CHUNK, SSM_GROUPS, SSM_STATE)
    cm = cm.reshape(b, c, CHUNK, SSM_GROUPS, SSM_STATE)
    a_dt = dt * a
    xdt = x * dt[..., None]
    a_cs = jnp.cumsum(a_dt, axis=2)
    cs = jnp.moveaxis(a_cs, 2, -1)
    tri = jnp.tril(jnp.ones((CHUNK, CHUNK), dtype=bool))
    decay = jnp.exp(jnp.where(tri, cs[..., :, None] - cs[..., None, :], -jnp.inf))
    cb = jnp.einsum('bclgn,bcsgn->bcgls', cm, bm)
    y_diag = jnp.einsum('bcgls,bcgrls,bcsgrp->bclgrp', cb, decay, xdt)
    decay_to_end = jnp.exp(a_cs[:, :, -1:] - a_cs)
    states = jnp.einsum('bcsgn,bcsgr,bcsgrp->bcgrpn', bm, decay_to_end, xdt)
    chunk_decay = jnp.exp(a_cs[:, :, -1])

    def step(h, inp):
        st, dec = inp
        return h * dec[..., None, None] + st, h

    h0 = jnp.zeros((b, SSM_GROUPS, SSM_HPG, SSM_HEAD_DIM, SSM_STATE), dtype=states.dtype)
    _, h_in = lax.scan(step, h0, (jnp.moveaxis(states, 1, 0), jnp.moveaxis(chunk_decay, 1, 0)))
    h_in = jnp.moveaxis(h_in, 0, 1)
    y_off = jnp.einsum('bclgn,bcgrpn,bclgr->bclgrp', cm, h_in, jnp.exp(a_cs))
    return (y_diag + y_off).reshape(b, s, SSM_GROUPS, SSM_HPG, SSM_HEAD_DIM)


def ssd_mixer(z, xbc, dt_raw, conv_w, conv_b, dt_bias, a_log, d_skip, norm_g):
    b, s, _ = z.shape
    f32 = jnp.float32
    xbc = jax.nn.silu(causal_dwconv(xbc, conv_w, conv_b))
    xs, bm, cm = jnp.split(xbc, [D_INNER, D_INNER + SSM_GROUPS * SSM_STATE], axis=-1)
    xs = xs.reshape(b, s, SSM_GROUPS, SSM_HPG, SSM_HEAD_DIM).astype(f32)
    bm = bm.reshape(b, s, SSM_GROUPS, SSM_STATE).astype(f32)
    cm = cm.reshape(b, s, SSM_GROUPS, SSM_STATE).astype(f32)
    dt = jax.nn.softplus(dt_raw.astype(f32) + dt_bias.astype(f32)).reshape(b, s, SSM_GROUPS, SSM_HPG)
    a = -jnp.exp(a_log.astype(f32)).reshape(SSM_GROUPS, SSM_HPG)
    y = ssd_chunked(xs, dt, a, bm, cm)
    y = y + d_skip.astype(f32).reshape(SSM_GROUPS, SSM_HPG)[:, :, None] * xs
    y = y.reshape(b, s, D_INNER) * jax.nn.silu(z.astype(f32))
    yg = y.reshape(b, s, SSM_GROUPS, D_INNER // SSM_GROUPS)
    yg = yg * lax.rsqrt(jnp.mean(yg * yg, axis=-1, keepdims=True) + EPS)
    y = yg.reshape(b, s, D_INNER) * norm_g.astype(f32)
    return y.astype(z.dtype)


def dsa_mixer(q, k, v, q_idx, k_idx, w_idx):
    b, s, _ = q.shape
    f32 = jnp.float32
    top_k = min(TOPK_MAX, s // 4)
    nb = s // Q_BLOCK
    q = q.reshape(b, nb, Q_BLOCK, ATTN_KV_HEADS, ATTN_QPK, ATTN_HEAD_DIM)
    k = k.reshape(b, s, ATTN_KV_HEADS, ATTN_HEAD_DIM)
    v = v.reshape(b, s, ATTN_KV_HEADS, ATTN_HEAD_DIM)
    q_idx = q_idx.reshape(b, nb, Q_BLOCK, IDX_HEADS, IDX_HEAD_DIM)
    w_idx = w_idx.reshape(b, nb, Q_BLOCK, IDX_HEADS)
    k_idx32 = k_idx.astype(f32)
    key_pos = jnp.arange(s)

    def block(args):
        blk, qb, qib, wib = args
        qpos = blk * Q_BLOCK + jnp.arange(Q_BLOCK)
        limit = (qpos // CHUNK + 1) * CHUNK
        admissible = key_pos[None, :] < limit[:, None]
        rel = jax.nn.relu(jnp.einsum('bqhd,bkd->bqhk', qib.astype(f32), k_idx32))
        score = jnp.einsum('bqhk,bqh->bqk', rel, wib.astype(f32)) * IDX_SCALE
        score = jnp.where(admissible[None], score, -jnp.inf)
        top_val, top_idx = lax.top_k(score, top_k)
        valid = jnp.isfinite(top_val)
        kg = jax.vmap(lambda kb, ib: kb[ib])(k, top_idx)
        vg = jax.vmap(lambda vb, ib: vb[ib])(v, top_idx)
        logits = jnp.einsum('bqhgd,bqkhd->bqhgk', qb, kg).astype(f32) * ATTN_SCALE
        logits = jnp.where(valid[:, :, None, None, :], logits, -jnp.inf)
        p = jax.nn.softmax(logits, axis=-1).astype(vg.dtype)
        out = jnp.einsum('bqhgk,bqkhd->bqhgd', p, vg)
        return out.reshape(b, Q_BLOCK, ATTN_Q_HEADS * ATTN_HEAD_DIM)

    xs = (jnp.arange(nb), jnp.moveaxis(q, 1, 0), jnp.moveaxis(q_idx, 1, 0), jnp.moveaxis(w_idx, 1, 0))
    out = lax.map(block, xs)
    return jnp.moveaxis(out, 0, 1).reshape(b, s, ATTN_Q_HEADS * ATTN_HEAD_DIM)


def conv_ffn(h, w_up, conv_w, conv_b, w_down):
    u = causal_dwconv(h @ w_up, conv_w, conv_b)
    g, val = jnp.split(u, 2, axis=-1)
    return (jax.nn.gelu(g, approximate=True) * val) @ w_down


def setup_inputs(seed: int = 0) -> dict:
    key = jax.random.key(seed)
    ks = jax.random.split(key, 24)
    f32 = jnp.float32
    nrm = lambda k, shape, fan_in: jax.random.normal(k, shape, f32) * (fan_in ** -0.5)
    gain = lambda k: 1.0 + 0.02 * jax.random.normal(k, (DEPTH, D_MODEL), f32)
    dt0 = jnp.exp(jax.random.uniform(ks[9], (DEPTH, SSM_HEADS), f32) * (math.log(0.1) - math.log(0.001)) + math.log(0.001))
    dt_bias = dt0 + jnp.log(-jnp.expm1(-dt0))
    a_log = jnp.log(jax.random.uniform(ks[10], (DEPTH, SSM_HEADS), f32, 1.0, 16.0))
    return {
        "x": jax.random.normal(ks[0], (BATCH, SEQ, D_MODEL), f32),
        "norm_mix_pre": gain(ks[1]),
        "norm_mix_post": gain(ks[2]),
        "norm_ffn_pre": gain(ks[3]),
        "norm_ffn_post": gain(ks[4]),
        "w_in": nrm(ks[5], (DEPTH, D_MODEL, D_IN_PROJ), D_MODEL),
        "b_gate": 0.01 * jax.random.normal(ks[6], (DEPTH, N_BRANCHES * D_MODEL), f32),
        "conv_xbc_w": nrm(ks[7], (DEPTH, SSM_CONV, CONV_DIM), SSM_CONV),
        "conv_xbc_b": 0.01 * jax.random.normal(ks[8], (DEPTH, CONV_DIM), f32),
        "dt_bias": dt_bias,
        "a_log": a_log,
        "d_skip": 1.0 + 0.02 * jax.random.normal(ks[11], (DEPTH, SSM_HEADS), f32),
        "ssm_norm": 1.0 + 0.02 * jax.random.normal(ks[12], (DEPTH, D_INNER), f32),
        "w_ssm_out": nrm(ks[13], (DEPTH, D_INNER, D_MODEL), D_INNER),
        "w_attn_out": nrm(ks[14], (DEPTH, ATTN_Q_HEADS * ATTN_HEAD_DIM, D_MODEL), ATTN_Q_HEADS * ATTN_HEAD_DIM),
        "w_mix_out": nrm(ks[15], (DEPTH, D_MODEL, D_MODEL), D_MODEL),
        "w_up": nrm(ks[16], (DEPTH, D_MODEL, 2 * D_FF), D_MODEL),
        "conv_ffn_w": nrm(ks[17], (DEPTH, FFN_CONV, 2 * D_FF), FFN_CONV),
        "conv_ffn_b": 0.01 * jax.random.normal(ks[18], (DEPTH, 2 * D_FF), f32),
        "w_down": nrm(ks[19], (DEPTH, D_FF, D_MODEL), D_FF),
    }


def reference(x, norm_mix_pre, norm_mix_post, norm_ffn_pre, norm_ffn_post, w_in, b_gate,
              conv_xbc_w, conv_xbc_b, dt_bias, a_log, d_skip, ssm_norm, w_ssm_out, w_attn_out,
              w_mix_out, w_up, conv_ffn_w, conv_ffn_b, w_down):
    for l in range(DEPTH):
        h = rmsnorm(x, norm_mix_pre[l])
        proj = h @ w_in[l]
        z, xbc, dt_raw, q, k, v, q_idx, k_idx, w_idx, gate_logits = split_in_proj(proj)
        y_ssm = ssd_mixer(z, xbc, dt_raw, conv_xbc_w[l], conv_xbc_b[l], dt_bias[l], a_log[l],
                          d_skip[l], ssm_norm[l]) @ w_ssm_out[l]
        y_attn = dsa_mixer(q, k, v, q_idx, k_idx, w_idx) @ w_attn_out[l]
        g_ssm, g_attn = jnp.split(jax.nn.sigmoid(gate_logits + b_gate[l]), N_BRANCHES, axis=-1)
        merged = g_ssm * y_ssm + g_attn * y_attn
        x = x + rmsnorm(merged @ w_mix_out[l], norm_mix_post[l])
        h = rmsnorm(x, norm_ffn_pre[l])
        x = x + rmsnorm(conv_ffn(h, w_up[l], conv_ffn_w[l], conv_ffn_b[l], w_down[l]), norm_ffn_post[l])
    return x
```

```python
import functools

import numpy as np
import jax
import jax.numpy as jnp
from jax import lax
from jax.experimental import pallas as pl
from jax.experimental.pallas import tpu as pltpu

F32 = jnp.float32
BF16 = jnp.bfloat16

D_MODEL = 2048
EPS = 1e-6

D_INNER = 2 * D_MODEL
SSM_HEAD_DIM = 64
SSM_HEADS = D_INNER // SSM_HEAD_DIM
SSM_GROUPS = 8
SSM_HPG = SSM_HEADS // SSM_GROUPS
SSM_STATE = 128
SSM_CONV = 4
CONV_DIM = D_INNER + 2 * SSM_GROUPS * SSM_STATE
GROUP_W = SSM_HPG * SSM_HEAD_DIM
SSD_T = 128

CHUNK = 64
ATTN_HEAD_DIM = 128
ATTN_Q_HEADS = 16
ATTN_KV_HEADS = 4
ATTN_QPK = ATTN_Q_HEADS // ATTN_KV_HEADS
ATTN_DIM = ATTN_Q_HEADS * ATTN_HEAD_DIM
KV_DIM = ATTN_KV_HEADS * ATTN_HEAD_DIM
ATTN_SCALE = ATTN_HEAD_DIM ** -0.5
IDX_HEADS = 16
IDX_HEAD_DIM = 64
IDX_DIM = IDX_HEADS * IDX_HEAD_DIM
IDX_SCALE = (IDX_HEAD_DIM ** -0.5) * (IDX_HEADS ** -0.5)
TOPK_MAX = 256
Q_BLOCK = 128
KEY_TILE = 512

N_BRANCHES = 2
D_FF = 2 * D_MODEL
FFN_CONV = 3

IN_SPLITS = (D_INNER, CONV_DIM, SSM_HEADS, ATTN_DIM, KV_DIM, KV_DIM,
             IDX_DIM, IDX_HEAD_DIM, IDX_HEADS, N_BRANCHES * D_MODEL)
_OFFS = [0] + [int(o) for o in np.cumsum(IN_SPLITS)]
OFF_Z, OFF_XBC, OFF_DT, OFF_Q, OFF_K, OFF_V, OFF_QIDX, OFF_KIDX, OFF_WIDX, OFF_GATE, D_IN_PROJ = _OFFS

LANES = 128
SUBLANES = 8
VMEM_LIMIT = 56 * 2 ** 20

INT_MIN = -2 ** 31
NEG_BIG = -1e30


def _cparams(*sem):
    return pltpu.CompilerParams(dimension_semantics=sem, vmem_limit_bytes=VMEM_LIMIT)


def _rms(x, g):
    return x * lax.rsqrt(jnp.mean(x * x, axis=-1, keepdims=True) + EPS) * g


def _norm_kernel(x_ref, g_ref, o_ref):
    o_ref[...] = _rms(x_ref[...], g_ref[...]).astype(o_ref.dtype)


def _norm(x, g, tm=512):
    m, d = x.shape
    return pl.pallas_call(
        _norm_kernel,
        grid=(m // tm,),
        in_specs=[pl.BlockSpec((tm, d), lambda i: (i, 0)),
                  pl.BlockSpec((1, d), lambda i: (0, 0))],
        out_specs=pl.BlockSpec((tm, d), lambda i: (i, 0)),
        out_shape=jax.ShapeDtypeStruct((m, d), BF16),
        compiler_params=_cparams("arbitrary"),
        name="rmsnorm",
    )(x, g.reshape(1, d))


def _load_w_window(wa_ref, wb_ref, wsc_ref, off, kc=256):
    k, tn = wsc_ref.shape

    def body(i, c):
        rows = pl.ds(pl.multiple_of(i * kc, kc), kc)
        if wb_ref is None:
            w = wa_ref[rows, :]
        else:
            wcat = jnp.concatenate([wa_ref[rows, :], wb_ref[rows, :]], axis=1)
            w = pltpu.roll(wcat, 2 * tn - off, axis=1)[:, :tn]
        wsc_ref[rows, :] = w.astype(BF16)
        return c

    lax.fori_loop(0, k // kc, body, 0)


def _proj_kernel(*refs, off, gate):
    refs = list(refs)
    a_ref = refs.pop(0)
    wa_ref = refs.pop(0)
    wb_ref = refs.pop(0) if off else None
    b_ref = refs.pop(0) if gate else None
    o_ref, wsc_ref = refs

    @pl.when(pl.program_id(1) == 0)
    def _():
        _load_w_window(wa_ref, wb_ref, wsc_ref, off)

    acc = jnp.dot(a_ref[...], wsc_ref[...], preferred_element_type=F32)
    if gate:
        acc = jax.nn.sigmoid(acc + b_ref[...])
    o_ref[...] = acc.astype(o_ref.dtype)


def _proj(h, w_all, layer, col_start, n_cols, out_dtype, bias=None, tn=512, tm=1024):
    m, k = h.shape
    tm = min(tm, m)
    off, j0, nt = col_start % tn, col_start // tn, n_cols // tn
    in_specs = [pl.BlockSpec((tm, k), lambda j, i: (i, 0)),
                pl.BlockSpec((None, k, tn), lambda j, i: (layer, 0, j0 + j))]
    args = [h, w_all]
    if off:
        in_specs.append(pl.BlockSpec((None, k, tn), lambda j, i: (layer, 0, j0 + j + 1)))
        args.append(w_all)
    if bias is not None:
        in_specs.append(pl.BlockSpec((1, tn), lambda j, i: (0, j)))
        args.append(bias.reshape(1, n_cols))
    return pl.pallas_call(
        functools.partial(_proj_kernel, off=off, gate=bias is not None),
        grid=(nt, m // tm),
        in_specs=in_specs,
        out_specs=pl.BlockSpec((tm, tn), lambda j, i: (i, j)),
        out_shape=jax.ShapeDtypeStruct((m, n_cols), out_dtype),
        scratch_shapes=[pltpu.VMEM((k, tn), BF16)],
        compiler_params=_cparams("arbitrary", "arbitrary"),
        name="in_proj",
    )(*args)


def _smalls_kernel(a_ref, wdt_ref, wk_ref, ww_ref, dt_ref, kidx_ref, widx_ref, *, k_lane0):
    a = a_ref[...]
    dt_ref[...] = jnp.dot(a, wdt_ref[...].astype(BF16), preferred_element_type=F32)
    widx_ref[...] = jnp.dot(a, ww_ref[...].astype(BF16), preferred_element_type=F32)
    r = jnp.dot(a, wk_ref[...].astype(BF16), preferred_element_type=F32)
    lane = lax.broadcasted_iota(jnp.int32, r.shape, 1)
    in_half = (lane >= k_lane0) & (lane < k_lane0 + IDX_HEAD_DIM)
    kidx_ref[...] = jnp.where(in_half, r, pltpu.roll(r, IDX_HEAD_DIM, axis=1)).astype(kidx_ref.dtype)


def _smalls(h, w_all, layer, tm=1024):
    m, k = h.shape
    tm = min(tm, m)
    assert OFF_DT % LANES == 0 and OFF_WIDX % LANES == 0
    assert OFF_KIDX % LANES in (0, IDX_HEAD_DIM) and 2 * IDX_HEAD_DIM == LANES
    wspec = lambda col: pl.BlockSpec((None, k, LANES), lambda i: (layer, 0, col // LANES))
    ospec = pl.BlockSpec((tm, LANES), lambda i: (i, 0))
    return pl.pallas_call(
        functools.partial(_smalls_kernel, k_lane0=OFF_KIDX % LANES),
        grid=(m // tm,),
        in_specs=[pl.BlockSpec((tm, k), lambda i: (i, 0)),
                  wspec(OFF_DT), wspec(OFF_KIDX), wspec(OFF_WIDX)],
        out_specs=[ospec, ospec, ospec],
        out_shape=[jax.ShapeDtypeStruct((m, LANES), F32),
                   jax.ShapeDtypeStruct((m, LANES), BF16),
                   jax.ShapeDtypeStruct((m, LANES), F32)],
        compiler_params=_cparams("arbitrary"),
        name="in_proj_narrow",
    )(h, w_all, w_all, w_all)


def _ssd_kernel(z_ref, xbc_ref, dt_ref, cw_ref, cb_ref, dtb_ref, alog_ref, dskip_ref, ng_ref,
                o_ref, xbuf_ref, xc_ref, st_ref):
    t = SSD_T
    pad = SUBLANES

    @pl.when(pl.program_id(1) == 0)
    def _():
        xbuf_ref[0:pad, :] = jnp.zeros((pad, CONV_DIM), F32)
        st_ref[...] = jnp.zeros_like(st_ref)

    xbuf_ref[pad:pad + t, :] = xbc_ref[...]
    cw = 512
    for c in range(CONV_DIM // cw):
        sl = slice(c * cw, (c + 1) * cw)
        acc = cb_ref[:, sl] + cw_ref[SSM_CONV - 1:SSM_CONV, sl] * xbuf_ref[pad:pad + t, sl]
        for i in range(1, SSM_CONV):
            acc = acc + cw_ref[SSM_CONV - 1 - i:SSM_CONV - i, sl] * xbuf_ref[pad - i:pad - i + t, sl]
        xc_ref[:, sl] = acc * jax.nn.sigmoid(acc)
    xbuf_ref[0:pad, :] = xbuf_ref[t:t + pad, :]

    lane = lax.broadcasted_iota(jnp.int32, (t, LANES), 1)
    row = lax.broadcasted_iota(jnp.int32, (t, LANES), 0)
    head_ok = lane < SSM_HEADS
    x_dt = dt_ref[...] + dtb_ref[...]
    dt = jnp.maximum(x_dt, 0.0) + jnp.log1p(jnp.exp(-jnp.abs(x_dt)))
    dt = jnp.where(head_ok, dt, 0.0)
    a_dt = dt * (-jnp.exp(alog_ref[...]))
    tril = row >= lane
    cs = jnp.dot(tril.astype(F32), a_dt, preferred_element_type=F32,
                 precision=lax.Precision.HIGHEST)
    cs_t = cs.T
    cs_last = cs[t - 1:t, :]
    dte = jnp.exp(cs_last - cs)
    expand_rows = jnp.concatenate(
        [dt, dt * dte, jnp.exp(cs), jnp.broadcast_to(jnp.exp(cs_last), (SUBLANES, LANES))], axis=0)

    gh = lax.broadcasted_iota(jnp.int32, (LANES, GROUP_W), 0)
    gc = lax.broadcasted_iota(jnp.int32, (LANES, GROUP_W), 1) >> (SSM_HEAD_DIM.bit_length() - 1)
    half = lax.broadcasted_iota(jnp.int32, (t, 2 * SSM_HEAD_DIM), 1) < SSM_HEAD_DIM

    for g in range(SSM_GROUPS):
        sl = slice(g * GROUP_W, (g + 1) * GROUP_W)
        expand = (gh == gc + g * SSM_HPG).astype(F32)
        ex = jnp.dot(expand_rows, expand, preferred_element_type=F32, precision=lax.Precision.HIGHEST)
        dt_e, dtdte_e, ecs_e, cd_e = ex[0:t], ex[t:2 * t], ex[2 * t:3 * t], ex[3 * t:3 * t + 1]
        xs = xc_ref[:, sl]
        bm = xc_ref[:, D_INNER + g * SSM_STATE:D_INNER + (g + 1) * SSM_STATE]
        cm = xc_ref[:, D_INNER + (SSM_GROUPS + g) * SSM_STATE:D_INNER + (SSM_GROUPS + g + 1) * SSM_STATE]
        bm16, cm16 = bm.astype(BF16), cm.astype(BF16)
        xdt = (xs * dt_e).astype(BF16)
        cb = lax.dot_general(cm16, bm16, (((1,), (1,)), ((), ())), preferred_element_type=F32)
        h_in = st_ref[:, sl]
        y = jnp.dot(cm16, h_in.astype(BF16), preferred_element_type=F32) * ecs_e
        new_states = jnp.dot(bm.T.astype(BF16), (xs * dtdte_e).astype(BF16), preferred_element_type=F32)
        st_ref[:, sl] = h_in * cd_e + new_states
        yd = []
        for j in range(SSM_HPG // 2):
            h1 = g * SSM_HPG + 2 * j
            ms = []
            for hh in (h1, h1 + 1):
                dec = jnp.exp(jnp.where(tril, cs[:, hh:hh + 1] - cs_t[hh:hh + 1, :], -jnp.inf))
                ms.append((cb * dec).astype(BF16))
            xp = xdt[:, 2 * j * SSM_HEAD_DIM:(2 * j + 2) * SSM_HEAD_DIM]
            zero = jnp.zeros_like(xp)
            blockdiag = jnp.concatenate([jnp.where(half, xp, zero), jnp.where(half, zero, xp)], axis=0)
            yd.append(jnp.dot(jnp.concatenate(ms, axis=1), blockdiag, preferred_element_type=F32))
        y = y + jnp.concatenate(yd, axis=1) + dskip_ref[:, sl] * xs
        zg = z_ref[:, sl]
        y = y * (zg * jax.nn.sigmoid(zg))
        y = y * lax.rsqrt(jnp.mean(y * y, axis=-1, keepdims=True) + EPS)
        o_ref[:, sl] = (y * ng_ref[:, sl]).astype(o_ref.dtype)


def _ssd(z, xbc, dt128, conv_w, conv_b, dt_bias, a_log, d_skip, norm_g, batch):
    m = z.shape[0]
    nblk = m // batch // SSD_T
    pad_heads = lambda v: jnp.pad(v, (0, LANES - SSM_HEADS)).reshape(1, LANES)
    row_spec = lambda w: pl.BlockSpec((SSD_T, w), lambda b, i: (b * nblk + i, 0))
    par_spec = lambda r, w: pl.BlockSpec((r, w), lambda b, i: (0, 0))
    return pl.pallas_call(
        _ssd_kernel,
        grid=(batch, nblk),
        in_specs=[row_spec(D_INNER), row_spec(CONV_DIM), row_spec(LANES),
                  par_spec(SSM_CONV, CONV_DIM), par_spec(1, CONV_DIM),
                  par_spec(1, LANES), par_spec(1, LANES), par_spec(1, D_INNER), par_spec(1, D_INNER)],
        out_specs=row_spec(D_INNER),
        out_shape=jax.ShapeDtypeStruct((m, D_INNER), BF16),
        scratch_shapes=[pltpu.VMEM((SSD_T + 2 * SUBLANES, CONV_DIM), F32),
                        pltpu.VMEM((SSD_T, CONV_DIM), F32),
                        pltpu.VMEM((SSM_STATE, D_INNER), F32)],
        compiler_params=_cparams("arbitrary", "arbitrary"),
        name="ssd_mixer",
    )(z, xbc, dt128, conv_w, conv_b.reshape(1, CONV_DIM), pad_heads(dt_bias), pad_heads(a_log),
      jnp.repeat(d_skip, SSM_HEAD_DIM).reshape(1, D_INNER), norm_g.reshape(1, D_INNER))


def _dsa_kernel(q_ref, qi_ref, w_ref, k_ref, v_ref, ki_ref, o_ref, keys_ref, bias_ref, qs_ref, *, top_k, seq):
    qb, kt = Q_BLOCK, KEY_TILE
    blk = pl.program_id(1)
    ntiles = (blk * qb + qb + kt - 1) // kt
    row1 = lax.broadcasted_iota(jnp.int32, (qb, 1), 0)
    limit = (((blk * qb + row1) >> (CHUNK.bit_length() - 1)) + 1) * CHUNK
    lane_t = lax.broadcasted_iota(jnp.int32, (qb, kt), 1)
    lane_q = lax.broadcasted_iota(jnp.int32, (qb, LANES), 1)

    def fold(x):
        acc = x[:, 0:LANES]
        for c in range(1, kt // LANES):
            acc = acc + x[:, c * LANES:(c + 1) * LANES]
        return acc

    def count(pred):
        def body(t, acc):
            return acc + fold(jnp.where(pred(t), 1.0, 0.0))
        acc = lax.fori_loop(0, ntiles, body, jnp.zeros((qb, LANES), F32))
        return jnp.sum(acc, axis=1, keepdims=True)

    w = w_ref[...]

    def score_tile(t, c):
        kx = ki_ref[pl.ds(pl.multiple_of(t * kt, kt), kt), :]
        acc = jnp.zeros((qb, kt), F32)
        for p in range(IDX_HEADS // 2):
            qp = qi_ref[:, p * LANES:(p + 1) * LANES]
            for e in range(2):
                sel = (lane_q < IDX_HEAD_DIM) if e == 0 else (lane_q >= IDX_HEAD_DIM)
                qh = jnp.where(sel, qp, jnp.zeros_like(qp))
                s = lax.dot_general(qh, kx, (((1,), (1,)), ((), ())), preferred_element_type=F32)
                hd = 2 * p + e
                acc = acc + jnp.maximum(s, 0.0) * w[:, hd:hd + 1]
        score = acc * IDX_SCALE + 0.0
        bits = pltpu.bitcast(score, jnp.int32)
        key = jnp.where(bits < 0, bits ^ jnp.int32(0x7FFFFFFF), bits)
        pos = t * kt + lane_t
        keys_ref[t] = jnp.where(pos < limit, key, jnp.int32(INT_MIN))
        return c

    lax.fori_loop(0, ntiles, score_tile, 0)

    def bit_step(it, tb):
        cand = tb | lax.shift_left(jnp.int32(1), jnp.int32(31) - it)
        cand_s = cand ^ jnp.int32(INT_MIN)
        cnt = count(lambda t: keys_ref[t] >= cand_s)
        return jnp.where(cnt >= float(top_k), cand, tb)

    tb = lax.fori_loop(0, 32, bit_step, jnp.zeros((qb, 1), jnp.int32))
    thr = tb ^ jnp.int32(INT_MIN)

    cnt_gt = count(lambda t: keys_ref[t] > thr)
    cnt_ge = cnt_gt + count(lambda t: keys_ref[t] == thr)
    excess = (cnt_ge > float(top_k)) & (thr != jnp.int32(INT_MIN))
    need = float(top_k) - cnt_gt

    def tie_cut():
        def step(it, mp):
            cand = mp | lax.shift_left(jnp.int32(1), jnp.int32(seq.bit_length() - 1) - it)
            cnt = count(lambda t: (keys_ref[t] == thr) & (t * kt + lane_t < cand))
            return jnp.where(cnt < need, cand, mp)
        return lax.fori_loop(0, seq.bit_length(), step, jnp.zeros((qb, 1), jnp.int32))

    any_excess = jnp.max(jnp.where(excess, 1.0, 0.0)) > 0.0
    last_tie = lax.cond(any_excess, tie_cut, lambda: jnp.full((qb, 1), seq, jnp.int32))

    def bias_tile(t, c):
        key = keys_ref[t]
        pos = t * kt + lane_t
        sel = (key > thr) | ((key == thr) & (pos <= last_tie))
        bias_ref[t] = jnp.where(sel & (pos < limit), 0.0, NEG_BIG)
        return c

    lax.fori_loop(0, ntiles, bias_tile, 0)

    for g in range(ATTN_KV_HEADS):
        for j in range(ATTN_QPK):
            hq = g * ATTN_QPK + j
            qs_ref[j * qb:(j + 1) * qb, :] = q_ref[:, hq * ATTN_HEAD_DIM:(hq + 1) * ATTN_HEAD_DIM]
        qg = qs_ref[...]
        rows = ATTN_QPK * qb

        def att_tile(t, carry):
            m_i, l_i, acc = carry
            ks = pl.ds(pl.multiple_of(t * kt, kt), kt)
            kx = k_ref[ks, g * ATTN_HEAD_DIM:(g + 1) * ATTN_HEAD_DIM]
            vx = v_ref[ks, g * ATTN_HEAD_DIM:(g + 1) * ATTN_HEAD_DIM]
            s = lax.dot_general(qg, kx, (((1,), (1,)), ((), ())), preferred_element_type=F32)
            s = (s * ATTN_SCALE).reshape(ATTN_QPK, qb, kt) + bias_ref[t][None]
            s = s.reshape(rows, kt)
            m_new = jnp.maximum(m_i, jnp.max(s, axis=1, keepdims=True))
            p = jnp.exp(s - m_new)
            alpha = jnp.exp(m_i - m_new)
            l_new = alpha * l_i + jnp.sum(p, axis=1, keepdims=True)
            acc = alpha * acc + jnp.dot(p.astype(BF16), vx, preferred_element_type=F32)
            return m_new, l_new, acc

        init = (jnp.full((rows, 1), NEG_BIG, F32), jnp.zeros((rows, 1), F32),
                jnp.zeros((rows, ATTN_HEAD_DIM), F32))
        _, l_i, acc = lax.fori_loop(0, ntiles, att_tile, init)
        out = acc / l_i
        for j in range(ATTN_QPK):
            hq = g * ATTN_QPK + j
            o_ref[:, hq * ATTN_HEAD_DIM:(hq + 1) * ATTN_HEAD_DIM] = out[j * qb:(j + 1) * qb].astype(o_ref.dtype)


def _dsa(qkvi, kidx2, widx, batch):
    m = qkvi.shape[0]
    seq = m // batch
    nb = seq // Q_BLOCK
    top_k = min(TOPK_MAX, seq // 4)
    c_k, c_v, c_qi = ATTN_DIM // KV_DIM, ATTN_DIM // KV_DIM + 1, (ATTN_DIM + 2 * KV_DIM) // IDX_DIM
    return pl.pallas_call(
        functools.partial(_dsa_kernel, top_k=top_k, seq=seq),
        grid=(batch, nb),
        in_specs=[pl.BlockSpec((Q_BLOCK, ATTN_DIM), lambda b, i: (b * nb + i, 0)),
                  pl.BlockSpec((Q_BLOCK, IDX_DIM), lambda b, i: (b * nb + i, c_qi)),
                  pl.BlockSpec((Q_BLOCK, LANES), lambda b, i: (b * nb + i, 0)),
                  pl.BlockSpec((seq, KV_DIM), lambda b, i: (b, c_k)),
                  pl.BlockSpec((seq, KV_DIM), lambda b, i: (b, c_v)),
                  pl.BlockSpec((seq, LANES), lambda b, i: (b, 0))],
        out_specs=pl.BlockSpec((Q_BLOCK, ATTN_DIM), lambda b, i: (b * nb + i, 0)),
        out_shape=jax.ShapeDtypeStruct((m, ATTN_DIM), BF16),
        scratch_shapes=[pltpu.VMEM((seq // KEY_TILE, Q_BLOCK, KEY_TILE), jnp.int32),
                        pltpu.VMEM((seq // KEY_TILE, Q_BLOCK, KEY_TILE), F32),
                        pltpu.VMEM((ATTN_QPK * Q_BLOCK, ATTN_HEAD_DIM), BF16)],
        compiler_params=_cparams("arbitrary", "arbitrary"),
        name="dsa_mixer",
    )(qkvi, qkvi, widx, qkvi, qkvi, kidx2)


def _merge_kernel(ys_ref, ya_ref, gs_ref, ga_ref, ws_ref, wa_ref, o_ref, wss_ref, was_ref):
    @pl.when(pl.program_id(1) == 0)
    def _():
        _load_w_window(ws_ref, None, wss_ref, 0)
        _load_w_window(wa_ref, None, was_ref, 0)

    ps = jnp.dot(ys_ref[...], wss_ref[...], preferred_element_type=F32)
    pa = jnp.dot(ya_ref[...], was_ref[...], preferred_element_type=F32)
    o_ref[...] = (gs_ref[...] * ps + ga_ref[...] * pa).astype(o_ref.dtype)


def _merge(y_ssm, y_attn, gates, w_ssm_out, w_attn_out, layer, tn=256, tm=512):
    m = y_ssm.shape[0]
    tm = min(tm, m)
    nt = D_MODEL // tn
    return pl.pallas_call(
        _merge_kernel,
        grid=(nt, m // tm),
        in_specs=[pl.BlockSpec((tm, D_INNER), lambda j, i: (i, 0)),
                  pl.BlockSpec((tm, ATTN_DIM), lambda j, i: (i, 0)),
                  pl.BlockSpec((tm, tn), lambda j, i: (i, j)),
                  pl.BlockSpec((tm, tn), lambda j, i: (i, nt + j)),
                  pl.BlockSpec((None, D_INNER, tn), lambda j, i: (layer, 0, j)),
                  pl.BlockSpec((None, ATTN_DIM, tn), lambda j, i: (layer, 0, j))],
        out_specs=pl.BlockSpec((tm, tn), lambda j, i: (i, j)),
        out_shape=jax.ShapeDtypeStruct((m, D_MODEL), BF16),
        scratch_shapes=[pltpu.VMEM((D_INNER, tn), BF16), pltpu.VMEM((ATTN_DIM, tn), BF16)],
        compiler_params=_cparams("arbitrary", "arbitrary"),
        name="branch_merge",
    )(y_ssm, y_attn, gates, gates, w_ssm_out, w_attn_out)


def _rowmm_kernel(a_ref, w_ref, x_ref, gpost_ref, gnext_ref, xo_ref, *rest, emit_next):
    ho_ref, acc_ref = rest if emit_next else (None, rest[0])
    kk = pl.program_id(1)

    @pl.when(kk == 0)
    def _():
        acc_ref[...] = jnp.zeros_like(acc_ref)

    acc_ref[...] += jnp.dot(a_ref[...], w_ref[...].astype(BF16), preferred_element_type=F32)

    @pl.when(kk == pl.num_programs(1) - 1)
    def _():
        x_new = x_ref[...] + _rms(acc_ref[...], gpost_ref[...])
        xo_ref[...] = x_new
        if emit_next:
            ho_ref[...] = _rms(x_new, gnext_ref[...]).astype(ho_ref.dtype)


def _rowmm(a, w_all, layer, x, g_post, g_next, tm=512, tk=512):
    m, k = a.shape
    d = x.shape[1]
    tm = min(tm, m)
    emit_next = g_next is not None
    if g_next is None:
        g_next = g_post
    row = pl.BlockSpec((tm, d), lambda i, kk: (i, 0))
    par = pl.BlockSpec((1, d), lambda i, kk: (0, 0))
    out_specs = [row, row] if emit_next else [row]
    out_shape = [jax.ShapeDtypeStruct((m, d), F32)] + ([jax.ShapeDtypeStruct((m, d), BF16)] if emit_next else [])
    outs = pl.pallas_call(
        functools.partial(_rowmm_kernel, emit_next=emit_next),
        grid=(m // tm, k // tk),
        in_specs=[pl.BlockSpec((tm, tk), lambda i, kk: (i, kk)),
                  pl.BlockSpec((None, tk, d), lambda i, kk: (layer, kk, 0)),
                  row, par, par],
        out_specs=out_specs,
        out_shape=out_shape,
        scratch_shapes=[pltpu.VMEM((tm, d), F32)],
        compiler_params=_cparams("arbitrary", "arbitrary"),
        name="out_proj_norm_residual",
    )(a, w_all, x, g_post.reshape(1, d), g_next.reshape(1, d))
    return (outs[0], outs[1]) if emit_next else (outs[0], None)


def _ffn_up_kernel(a_ref, wg_ref, wv_ref, cwg_ref, cwv_ref, cbg_ref, cbv_ref, o_ref,
                   wgs_ref, wvs_ref, ug_ref, uv_ref, *, tiles_per_seq):
    pad = SUBLANES
    tm = a_ref.shape[0]
    i = pl.program_id(1)

    @pl.when(i == 0)
    def _():
        _load_w_window(wg_ref, None, wgs_ref, 0)
        _load_w_window(wv_ref, None, wvs_ref, 0)

    @pl.when(i % tiles_per_seq == 0)
    def _():
        ug_ref[0:pad, :] = jnp.zeros((pad, ug_ref.shape[1]), F32)
        uv_ref[0:pad, :] = jnp.zeros((pad, uv_ref.shape[1]), F32)

    a = a_ref[...]

    def conv(w_s, u_ref, cw_ref, cb_ref):
        u_ref[pad:pad + tm, :] = jnp.dot(a, w_s[...], preferred_element_type=F32)
        acc = cb_ref[...] + cw_ref[FFN_CONV - 1:FFN_CONV, :] * u_ref[pad:pad + tm, :]
        for s in range(1, FFN_CONV):
            acc = acc + cw_ref[FFN_CONV - 1 - s:FFN_CONV - s, :] * u_ref[pad - s:pad - s + tm, :]
        u_ref[0:pad, :] = u_ref[tm:tm + pad, :]
        return acc

    gte = conv(wgs_ref, ug_ref, cwg_ref, cbg_ref)
    val = conv(wvs_ref, uv_ref, cwv_ref, cbv_ref)
    o_ref[...] = (jax.nn.gelu(gte, approximate=True) * val).astype(o_ref.dtype)


def _ffn_up(h, w_up, conv_w, conv_b, layer, batch, tn=512, tm=1024):
    m, k = h.shape
    seq = m // batch
    tm = min(tm, seq)
    nt = D_FF // tn
    wspec = lambda shift: pl.BlockSpec((None, k, tn), lambda j, i: (layer, 0, j + shift))
    cwspec = lambda shift: pl.BlockSpec((FFN_CONV, tn), lambda j, i: (0, j + shift))
    cbspec = lambda shift: pl.BlockSpec((1, tn), lambda j, i: (0, j + shift))
    return pl.pallas_call(
        functools.partial(_ffn_up_kernel, tiles_per_seq=seq // tm),
        grid=(nt, m // tm),
        in_specs=[pl.BlockSpec((tm, k), lambda j, i: (i, 0)),
                  wspec(0), wspec(nt), cwspec(0), cwspec(nt), cbspec(0), cbspec(nt)],
        out_specs=pl.BlockSpec((tm, tn), lambda j, i: (i, j)),
        out_shape=jax.ShapeDtypeStruct((m, D_FF), BF16),
        scratch_shapes=[pltpu.VMEM((k, tn), BF16), pltpu.VMEM((k, tn), BF16),
                        pltpu.VMEM((tm + 2 * SUBLANES, tn), F32), pltpu.VMEM((tm + 2 * SUBLANES, tn), F32)],
        compiler_params=_cparams("arbitrary", "arbitrary"),
        name="ffn_up_conv_geglu",
    )(h, w_up, w_up, conv_w, conv_w, conv_b.reshape(1, 2 * D_FF), conv_b.reshape(1, 2 * D_FF))


def kernel(x, norm_mix_pre, norm_mix_post, norm_ffn_pre, norm_ffn_post, w_in, b_gate, conv_xbc_w, conv_xbc_b, dt_bias, a_log, d_skip, ssm_norm, w_ssm_out, w_attn_out, w_mix_out, w_up, conv_ffn_w, conv_ffn_b, w_down):
    batch, seq, d = x.shape
    depth = w_in.shape[0]
    xf = x.reshape(batch * seq, d)
    h = _norm(xf, norm_mix_pre[0])
    for l in range(depth):
        z = _proj(h, w_in, l, OFF_Z, D_INNER, F32)
        xbc = _proj(h, w_in, l, OFF_XBC, CONV_DIM, F32)
        qkvi = _proj(h, w_in, l, OFF_Q, ATTN_DIM + 2 * KV_DIM + IDX_DIM, BF16)
        gates = _proj(h, w_in, l, OFF_GATE, N_BRANCHES * D_MODEL, F32, bias=b_gate[l])
        dt128, kidx2, widx = _smalls(h, w_in, l)
        y_ssm = _ssd(z, xbc, dt128, conv_xbc_w[l], conv_xbc_b[l], dt_bias[l], a_log[l], d_skip[l],
                     ssm_norm[l], batch)
        y_attn = _dsa(qkvi, kidx2, widx, batch)
        merged = _merge(y_ssm, y_attn, gates, w_ssm_out, w_attn_out, l)
        xf, h2 = _rowmm(merged, w_mix_out, l, xf, norm_mix_post[l], norm_ffn_pre[l])
        act = _ffn_up(h2, w_up, conv_ffn_w[l], conv_ffn_b[l], l, batch)
        xf, h = _rowmm(act, w_down, l, xf, norm_ffn_post[l], norm_mix_pre[l + 1] if l + 1 < depth else None)
    return xf.reshape(batch, seq, d)
```

```python
import functools

import numpy as np
import jax
import jax.numpy as jnp
from jax import lax
from jax.experimental import pallas as pl
from jax.experimental.pallas import tpu as pltpu

F32 = jnp.float32
BF16 = jnp.bfloat16

D_MODEL = 2048
EPS = 1e-6

D_INNER = 2 * D_MODEL
SSM_HEAD_DIM = 64
SSM_HEADS = D_INNER // SSM_HEAD_DIM
SSM_GROUPS = 8
SSM_HPG = SSM_HEADS // SSM_GROUPS
SSM_STATE = 128
SSM_CONV = 4
CONV_DIM = D_INNER + 2 * SSM_GROUPS * SSM_STATE
GROUP_W = SSM_HPG * SSM_HEAD_DIM
SSD_T = 128

CHUNK = 64
ATTN_HEAD_DIM = 128
ATTN_Q_HEADS = 16
ATTN_KV_HEADS = 4
ATTN_QPK = ATTN_Q_HEADS // ATTN_KV_HEADS
ATTN_DIM = ATTN_Q_HEADS * ATTN_HEAD_DIM
KV_DIM = ATTN_KV_HEADS * ATTN_HEAD_DIM
ATTN_SCALE = ATTN_HEAD_DIM ** -0.5
IDX_HEADS = 16
IDX_HEAD_DIM = 64
IDX_DIM = IDX_HEADS * IDX_HEAD_DIM
IDX_SCALE = (IDX_HEAD_DIM ** -0.5) * (IDX_HEADS ** -0.5)
TOPK_MAX = 256
Q_BLOCK = 128
KEY_TILE = 512

N_BRANCHES = 2
D_FF = 2 * D_MODEL
FFN_CONV = 3

IN_SPLITS = (D_INNER, CONV_DIM, SSM_HEADS, ATTN_DIM, KV_DIM, KV_DIM,
             IDX_DIM, IDX_HEAD_DIM, IDX_HEADS, N_BRANCHES * D_MODEL)
_OFFS = [0] + [int(o) for o in np.cumsum(IN_SPLITS)]
OFF_Z, OFF_XBC, OFF_DT, OFF_Q, OFF_K, OFF_V, OFF_QIDX, OFF_KIDX, OFF_WIDX, OFF_GATE, D_IN_PROJ = _OFFS

LANES = 128
SUBLANES = 8
VMEM_LIMIT = 56 * 2 ** 20

INT_MIN = -2 ** 31
NEG_BIG = -1e30


def _cparams(*sem):
    return pltpu.CompilerParams(dimension_semantics=sem, vmem_limit_bytes=VMEM_LIMIT)


def _rms(x, g):
    return x * lax.rsqrt(jnp.mean(x * x, axis=-1, keepdims=True) + EPS) * g


def _norm_kernel(x_ref, g_ref, o_ref):
    o_ref[...] = _rms(x_ref[...], g_ref[...]).astype(o_ref.dtype)


def _norm(x, g, tm=512):
    m, d = x.shape
    return pl.pallas_call(
        _norm_kernel,
        grid=(m // tm,),
        in_specs=[pl.BlockSpec((tm, d), lambda i: (i, 0)),
                  pl.BlockSpec((1, d), lambda i: (0, 0))],
        out_specs=pl.BlockSpec((tm, d), lambda i: (i, 0)),
        out_shape=jax.ShapeDtypeStruct((m, d), BF16),
        compiler_params=_cparams("arbitrary"),
        name="rmsnorm",
    )(x, g.reshape(1, d))


def _cast_weight(w_ref, wsc_ref, kc=256):
    def body(i, c):
        rows = pl.ds(pl.multiple_of(i * kc, kc), kc)
        wsc_ref[rows, :] = w_ref[rows, :].astype(BF16)
        return c

    lax.fori_loop(0, wsc_ref.shape[0] // kc, body, 0)


def _dot_nt(a, b):
    return lax.dot_general(a, b, (((1,), (1,)), ((), ())), preferred_element_type=F32)


def _proj_kernel(*refs, off, gate, tm):
    refs = list(refs)
    h_ref = refs.pop(0)
    wa_ref = refs.pop(0)
    wb_ref = refs.pop(0) if off else None
    b_ref = refs.pop(0) if gate else None
    o_ref, wsc_ref = refs
    tn = wsc_ref.shape[0]
    i = pl.program_id(1)

    @pl.when(i == 0)
    def _():
        if off:
            wsc_ref[0:tn - off, :] = wa_ref[off:tn, :].astype(BF16)
            wsc_ref[tn - off:tn, :] = wb_ref[0:off, :].astype(BF16)
        else:
            wsc_ref[...] = wa_ref[...].astype(BF16)

    a = h_ref[pl.ds(pl.multiple_of(i * tm, tm), tm), :]
    acc = _dot_nt(a, wsc_ref[...])
    if gate:
        acc = jax.nn.sigmoid(acc + b_ref[...])
    o_ref[...] = acc.astype(o_ref.dtype)


def _proj(h, w_t, layer, row_start, n_rows, out_dtype, bias=None, tn=256, tm=2048):
    m, k = h.shape
    tm = min(tm, m)
    off, j0, nt = row_start % tn, row_start // tn, n_rows // tn
    assert off % 16 == 0 and n_rows % tn == 0
    in_specs = [pl.BlockSpec((m, k), lambda j, i: (0, 0)),
                pl.BlockSpec((None, tn, k), lambda j, i: (layer, j0 + j, 0))]
    args = [h, w_t]
    if off:
        in_specs.append(pl.BlockSpec((None, tn, k), lambda j, i: (layer, j0 + j + 1, 0)))
        args.append(w_t)
    if bias is not None:
        in_specs.append(pl.BlockSpec((1, tn), lambda j, i: (0, j)))
        args.append(bias.reshape(1, n_rows))
    return pl.pallas_call(
        functools.partial(_proj_kernel, off=off, gate=bias is not None, tm=tm),
        grid=(nt, m // tm),
        in_specs=in_specs,
        out_specs=pl.BlockSpec((tm, tn), lambda j, i: (i, j)),
        out_shape=jax.ShapeDtypeStruct((m, n_rows), out_dtype),
        scratch_shapes=[pltpu.VMEM((tn, k), BF16)],
        compiler_params=_cparams("arbitrary", "arbitrary"),
        name="in_proj",
    )(*args)


def _smalls_kernel(a_ref, wdt_ref, wk_ref, ww_ref, dt_ref, kidx_ref, widx_ref):
    a = a_ref[...]
    k = a.shape[1]

    def pad_rows(w):
        return jnp.concatenate([w, jnp.zeros((LANES - w.shape[0], k), BF16)], axis=0)

    wk = wk_ref[...].astype(BF16)
    dt_ref[...] = _dot_nt(a, pad_rows(wdt_ref[...].astype(BF16)))
    widx_ref[...] = _dot_nt(a, pad_rows(ww_ref[...].astype(BF16)))
    kidx_ref[...] = _dot_nt(a, jnp.concatenate([wk, wk], axis=0)).astype(kidx_ref.dtype)


def _smalls(h, w_t, layer, tm=1024):
    m, k = h.shape
    tm = min(tm, m)
    assert 2 * IDX_HEAD_DIM == LANES and IDX_HEADS % 16 == 0
    wspec = lambda start, n: pl.BlockSpec((None, n, k), lambda i: (layer, start // n, 0))
    ospec = pl.BlockSpec((tm, LANES), lambda i: (i, 0))
    return pl.pallas_call(
        _smalls_kernel,
        grid=(m // tm,),
        in_specs=[pl.BlockSpec((tm, k), lambda i: (i, 0)),
                  wspec(OFF_DT, SSM_HEADS), wspec(OFF_KIDX, IDX_HEAD_DIM), wspec(OFF_WIDX, IDX_HEADS)],
        out_specs=[ospec, ospec, ospec],
        out_shape=[jax.ShapeDtypeStruct((m, LANES), F32),
                   jax.ShapeDtypeStruct((m, LANES), BF16),
                   jax.ShapeDtypeStruct((m, LANES), F32)],
        compiler_params=_cparams("arbitrary"),
        name="in_proj_narrow",
    )(h, w_t, w_t, w_t)


def _ssd_kernel(z_ref, xbc_ref, dt_ref, cw_ref, cb_ref, dtb_ref, alog_ref, dskip_ref, ng_ref,
                o_ref, xbuf_ref, xc_ref, st_ref):
    t = SSD_T
    pad = SUBLANES

    @pl.when(pl.program_id(1) == 0)
    def _():
        xbuf_ref[0:pad, :] = jnp.zeros((pad, CONV_DIM), F32)
        st_ref[...] = jnp.zeros_like(st_ref)

    xbuf_ref[pad:pad + t, :] = xbc_ref[...]
    cw = 512
    for c in range(CONV_DIM // cw):
        sl = slice(c * cw, (c + 1) * cw)
        acc = cb_ref[:, sl] + cw_ref[SSM_CONV - 1:SSM_CONV, sl] * xbuf_ref[pad:pad + t, sl]
        for i in range(1, SSM_CONV):
            acc = acc + cw_ref[SSM_CONV - 1 - i:SSM_CONV - i, sl] * xbuf_ref[pad - i:pad - i + t, sl]
        xc_ref[:, sl] = acc * jax.nn.sigmoid(acc)
    xbuf_ref[0:pad, :] = xbuf_ref[t:t + pad, :]

    lane = lax.broadcasted_iota(jnp.int32, (t, LANES), 1)
    row = lax.broadcasted_iota(jnp.int32, (t, LANES), 0)
    head_ok = lane < SSM_HEADS
    x_dt = dt_ref[...] + dtb_ref[...]
    dt = jnp.maximum(x_dt, 0.0) + jnp.log1p(jnp.exp(-jnp.abs(x_dt)))
    dt = jnp.where(head_ok, dt, 0.0)
    a_dt = dt * (-jnp.exp(alog_ref[...]))
    tril = row >= lane
    cs = jnp.dot(tril.astype(F32), a_dt, preferred_element_type=F32,
                 precision=lax.Precision.HIGHEST)
    cs_t = cs.T
    dt_t = dt.T
    cs_last = cs[t - 1:t, :]
    dte = jnp.exp(cs_last - cs)
    expand_rows = jnp.concatenate(
        [dt * dte, jnp.exp(cs), jnp.broadcast_to(jnp.exp(cs_last), (SUBLANES, LANES))], axis=0)
    hi = expand_rows.astype(BF16)
    rem = expand_rows - hi.astype(F32)
    mid = rem.astype(BF16)
    lo = (rem - mid.astype(F32)).astype(BF16)
    expand_rows3 = jnp.concatenate([hi, mid, lo], axis=1)

    gh = lax.broadcasted_iota(jnp.int32, (LANES, GROUP_W), 0)
    gc = lax.broadcasted_iota(jnp.int32, (LANES, GROUP_W), 1) >> (SSM_HEAD_DIM.bit_length() - 1)
    half = lax.broadcasted_iota(jnp.int32, (t, 2 * SSM_HEAD_DIM), 1) < SSM_HEAD_DIM

    for g in range(SSM_GROUPS):
        sl = slice(g * GROUP_W, (g + 1) * GROUP_W)
        expand = jnp.where(gh == gc + g * SSM_HPG, 1.0, 0.0).astype(BF16)
        ex = jnp.dot(expand_rows3, jnp.concatenate([expand, expand, expand], axis=0),
                     preferred_element_type=F32)
        dtdte_e, ecs_e, cd_e = ex[0:t], ex[t:2 * t], ex[2 * t:2 * t + 1]
        xs = xc_ref[:, sl]
        bm = xc_ref[:, D_INNER + g * SSM_STATE:D_INNER + (g + 1) * SSM_STATE]
        cm = xc_ref[:, D_INNER + (SSM_GROUPS + g) * SSM_STATE:D_INNER + (SSM_GROUPS + g + 1) * SSM_STATE]
        bm16, cm16 = bm.astype(BF16), cm.astype(BF16)
        xs16 = xs.astype(BF16)
        cb = _dot_nt(cm16, bm16)
        h_in = st_ref[:, sl]
        y = jnp.dot(cm16, h_in.astype(BF16), preferred_element_type=F32) * ecs_e
        new_states = jnp.dot(bm.T.astype(BF16), (xs * dtdte_e).astype(BF16), preferred_element_type=F32)
        st_ref[:, sl] = h_in * cd_e + new_states
        yd = []
        for j in range(SSM_HPG // 2):
            h1 = g * SSM_HPG + 2 * j
            ms = []
            for hh in (h1, h1 + 1):
                dec = jnp.exp(jnp.where(tril, cs[:, hh:hh + 1] - cs_t[hh:hh + 1, :], -jnp.inf))
                ms.append((cb * (dec * dt_t[hh:hh + 1, :])).astype(BF16))
            xp = xs16[:, 2 * j * SSM_HEAD_DIM:(2 * j + 2) * SSM_HEAD_DIM]
            zero = jnp.zeros_like(xp)
            blockdiag = jnp.concatenate([jnp.where(half, xp, zero), jnp.where(half, zero, xp)], axis=0)
            yd.append(jnp.dot(jnp.concatenate(ms, axis=1), blockdiag, preferred_element_type=F32))
        y = y + jnp.concatenate(yd, axis=1) + dskip_ref[:, sl] * xs
        zg = z_ref[:, sl]
        y = y * (zg * jax.nn.sigmoid(zg))
        y = y * lax.rsqrt(jnp.mean(y * y, axis=-1, keepdims=True) + EPS)
        o_ref[:, sl] = (y * ng_ref[:, sl]).astype(o_ref.dtype)


def _ssd(z, xbc, dt128, conv_w, conv_b, dt_bias, a_log, d_skip, norm_g, batch):
    m = z.shape[0]
    nblk = m // batch // SSD_T
    pad_heads = lambda v: jnp.pad(v, (0, LANES - SSM_HEADS)).reshape(1, LANES)
    row_spec = lambda w: pl.BlockSpec((SSD_T, w), lambda b, i: (b * nblk + i, 0))
    par_spec = lambda r, w: pl.BlockSpec((r, w), lambda b, i: (0, 0))
    return pl.pallas_call(
        _ssd_kernel,
        grid=(batch, nblk),
        in_specs=[row_spec(D_INNER), row_spec(CONV_DIM), row_spec(LANES),
                  par_spec(SSM_CONV, CONV_DIM), par_spec(1, CONV_DIM),
                  par_spec(1, LANES), par_spec(1, LANES), par_spec(1, D_INNER), par_spec(1, D_INNER)],
        out_specs=row_spec(D_INNER),
        out_shape=jax.ShapeDtypeStruct((m, D_INNER), BF16),
        scratch_shapes=[pltpu.VMEM((SSD_T + 2 * SUBLANES, CONV_DIM), F32),
                        pltpu.VMEM((SSD_T, CONV_DIM), F32),
                        pltpu.VMEM((SSM_STATE, D_INNER), F32)],
        compiler_params=_cparams("arbitrary", "arbitrary"),
        name="ssd_mixer",
    )(z, xbc, dt128, conv_w, conv_b.reshape(1, CONV_DIM), pad_heads(dt_bias), pad_heads(a_log),
      jnp.repeat(d_skip, SSM_HEAD_DIM).reshape(1, D_INNER), norm_g.reshape(1, D_INNER))


def _dsa_kernel(q_ref, qi_ref, w_ref, k_ref, v_ref, ki_ref, o_ref,
                keys_ref, bias_ref, qs_ref, qh_ref, *, top_k, seq):
    qb, kt = Q_BLOCK, KEY_TILE
    blk = pl.program_id(1)
    ntiles = (blk * qb + qb + kt - 1) // kt
    lane_q = lax.broadcasted_iota(jnp.int32, (1, qb), 1)
    limit = (((blk * qb + lane_q) >> (CHUNK.bit_length() - 1)) + 1) * CHUNK
    pos_t = lax.broadcasted_iota(jnp.int32, (kt, qb), 0)
    fold_rows = 64

    def count(pred):
        def body(t, acc):
            x = jnp.where(pred(t), 1.0, 0.0)
            return acc + jnp.sum(x.reshape(kt // fold_rows, fold_rows, qb), axis=0)
        acc = lax.fori_loop(0, ntiles, body, jnp.zeros((fold_rows, qb), F32))
        return jnp.sum(acc, axis=0, keepdims=True)

    w_t = w_ref[...].T
    lane_d = lax.broadcasted_iota(jnp.int32, (qb, LANES), 1)
    for p in range(IDX_HEADS // 2):
        qp = qi_ref[:, p * LANES:(p + 1) * LANES]
        zero = jnp.zeros_like(qp)
        qh_ref[p, 0:qb, :] = jnp.where(lane_d < IDX_HEAD_DIM, qp, zero)
        qh_ref[p, qb:2 * qb, :] = jnp.where(lane_d < IDX_HEAD_DIM, zero, qp)

    def score_tile(t, c):
        kx = ki_ref[pl.ds(pl.multiple_of(t * kt, kt), kt), :]
        acc = jnp.zeros((kt, qb), F32)
        for p in range(IDX_HEADS // 2):
            s2 = _dot_nt(kx, qh_ref[p])
            for e in range(2):
                hd = 2 * p + e
                acc = acc + jnp.maximum(s2[:, e * qb:(e + 1) * qb], 0.0) * w_t[hd:hd + 1, :]
        score = acc * IDX_SCALE + 0.0
        bits = pltpu.bitcast(score, jnp.int32)
        key = jnp.where(bits < 0, bits ^ jnp.int32(0x7FFFFFFF), bits)
        keys_ref[t] = jnp.where(t * kt + pos_t < limit, key, jnp.int32(INT_MIN))
        return c

    lax.fori_loop(0, ntiles, score_tile, 0)

    def bit_step(it, tb):
        cand = tb | lax.shift_left(jnp.int32(1), jnp.int32(31) - it)
        cand_s = cand ^ jnp.int32(INT_MIN)
        cnt = count(lambda t: keys_ref[t] >= cand_s)
        return jnp.where(cnt >= float(top_k), cand, tb)

    tb = lax.fori_loop(0, 32, bit_step, jnp.zeros((1, qb), jnp.int32))
    thr = tb ^ jnp.int32(INT_MIN)

    cnt_gt = count(lambda t: keys_ref[t] > thr)
    cnt_ge = cnt_gt + count(lambda t: keys_ref[t] == thr)
    excess = (cnt_ge > float(top_k)) & (thr != jnp.int32(INT_MIN))
    need = float(top_k) - cnt_gt

    def tie_cut():
        def step(it, mp):
            cand = mp | lax.shift_left(jnp.int32(1), jnp.int32(seq.bit_length() - 1) - it)
            cnt = count(lambda t: (keys_ref[t] == thr) & (t * kt + pos_t < cand))
            return jnp.where(cnt < need, cand, mp)
        return lax.fori_loop(0, seq.bit_length(), step, jnp.zeros((1, qb), jnp.int32))

    any_excess = jnp.max(jnp.where(excess, 1.0, 0.0)) > 0.0
    last_tie = lax.cond(any_excess, tie_cut, lambda: jnp.full((1, qb), seq, jnp.int32))

    def bias_tile(t, c):
        key = keys_ref[t]
        pos = t * kt + pos_t
        sel = (key > thr) | ((key == thr) & (pos <= last_tie))
        b_t = jnp.where(sel & (pos < limit), 0.0, NEG_BIG)
        for c4 in range(kt // LANES):
            bias_ref[t, :, c4 * LANES:(c4 + 1) * LANES] = b_t[c4 * LANES:(c4 + 1) * LANES, :].T
        return c

    lax.fori_loop(0, ntiles, bias_tile, 0)

    log2_scale = ATTN_SCALE * float(np.log2(np.e))
    for g in range(ATTN_KV_HEADS):
        for j in range(ATTN_QPK):
            hq = g * ATTN_QPK + j
            qs_ref[j * qb:(j + 1) * qb, :] = q_ref[:, hq * ATTN_HEAD_DIM:(hq + 1) * ATTN_HEAD_DIM]
        n_chain = 1
        hpc = ATTN_QPK // n_chain
        rows = hpc * qb

        def att_tile(t, carry):
            ks = pl.ds(pl.multiple_of(t * kt, kt), kt)
            kx = k_ref[ks, g * ATTN_HEAD_DIM:(g + 1) * ATTN_HEAD_DIM]
            vx = v_ref[ks, g * ATTN_HEAD_DIM:(g + 1) * ATTN_HEAD_DIM]
            bias = bias_ref[t][None]
            new = []
            for c in range(n_chain):
                m_i, l_i, acc = carry[c]
                s = _dot_nt(qs_ref[c * rows:(c + 1) * rows, :], kx)
                s = ((s * log2_scale).reshape(hpc, qb, kt) + bias).reshape(rows, kt)
                m_new = jnp.maximum(m_i, jnp.max(s, axis=1, keepdims=True))
                p = jnp.exp2(s - m_new)
                alpha = jnp.exp2(m_i - m_new)
                new.append((m_new, alpha * l_i + jnp.sum(p, axis=1, keepdims=True),
                            alpha * acc + jnp.dot(p.astype(BF16), vx, preferred_element_type=F32)))
            return tuple(new)

        init = tuple((jnp.full((rows, 1), NEG_BIG, F32), jnp.zeros((rows, 1), F32),
                      jnp.zeros((rows, ATTN_HEAD_DIM), F32)) for _ in range(n_chain))
        res = lax.fori_loop(0, ntiles, att_tile, init)
        for c in range(n_chain):
            out = res[c][2] / res[c][1]
            for j in range(hpc):
                hq = g * ATTN_QPK + c * hpc + j
                o_ref[:, hq * ATTN_HEAD_DIM:(hq + 1) * ATTN_HEAD_DIM] = out[j * qb:(j + 1) * qb].astype(o_ref.dtype)


def _dsa(qkvi, kidx2, widx, batch):
    m = qkvi.shape[0]
    seq = m // batch
    nb = seq // Q_BLOCK
    top_k = min(TOPK_MAX, seq // 4)
    c_k, c_v, c_qi = ATTN_DIM // KV_DIM, ATTN_DIM // KV_DIM + 1, (ATTN_DIM + 2 * KV_DIM) // IDX_DIM
    return pl.pallas_call(
        functools.partial(_dsa_kernel, top_k=top_k, seq=seq),
        grid=(batch, nb),
        in_specs=[pl.BlockSpec((Q_BLOCK, ATTN_DIM), lambda b, i: (b * nb + i, 0)),
                  pl.BlockSpec((Q_BLOCK, IDX_DIM), lambda b, i: (b * nb + i, c_qi)),
                  pl.BlockSpec((Q_BLOCK, LANES), lambda b, i: (b * nb + i, 0)),
                  pl.BlockSpec((seq, KV_DIM), lambda b, i: (b, c_k)),
                  pl.BlockSpec((seq, KV_DIM), lambda b, i: (b, c_v)),
                  pl.BlockSpec((seq, LANES), lambda b, i: (b, 0))],
        out_specs=pl.BlockSpec((Q_BLOCK, ATTN_DIM), lambda b, i: (b * nb + i, 0)),
        out_shape=jax.ShapeDtypeStruct((m, ATTN_DIM), BF16),
        scratch_shapes=[pltpu.VMEM((seq // KEY_TILE, KEY_TILE, Q_BLOCK), jnp.int32),
                        pltpu.VMEM((seq // KEY_TILE, Q_BLOCK, KEY_TILE), F32),
                        pltpu.VMEM((ATTN_QPK * Q_BLOCK, ATTN_HEAD_DIM), BF16),
                        pltpu.VMEM((IDX_HEADS // 2, 2 * Q_BLOCK, LANES), BF16)],
        compiler_params=_cparams("arbitrary", "arbitrary"),
        name="dsa_mixer",
    )(qkvi, qkvi, widx, qkvi, qkvi, kidx2)


def _merge_kernel(ys_ref, ya_ref, gs_ref, ga_ref, ws_ref, wa_ref, o_ref, wss_ref, was_ref):
    @pl.when(pl.program_id(1) == 0)
    def _():
        _cast_weight(ws_ref, wss_ref)
        _cast_weight(wa_ref, was_ref)

    ps = jnp.dot(ys_ref[...], wss_ref[...], preferred_element_type=F32)
    pa = jnp.dot(ya_ref[...], was_ref[...], preferred_element_type=F32)
    o_ref[...] = (gs_ref[...] * ps + ga_ref[...] * pa).astype(o_ref.dtype)


def _merge(y_ssm, y_attn, gates, w_ssm_out, w_attn_out, layer, tn=512, tm=512):
    m = y_ssm.shape[0]
    tm = min(tm, m)
    nt = D_MODEL // tn
    return pl.pallas_call(
        _merge_kernel,
        grid=(nt, m // tm),
        in_specs=[pl.BlockSpec((tm, D_INNER), lambda j, i: (i, 0)),
                  pl.BlockSpec((tm, ATTN_DIM), lambda j, i: (i, 0)),
                  pl.BlockSpec((tm, tn), lambda j, i: (i, j)),
                  pl.BlockSpec((tm, tn), lambda j, i: (i, nt + j)),
                  pl.BlockSpec((None, D_INNER, tn), lambda j, i: (layer, 0, j)),
                  pl.BlockSpec((None, ATTN_DIM, tn), lambda j, i: (layer, 0, j))],
        out_specs=pl.BlockSpec((tm, tn), lambda j, i: (i, j)),
        out_shape=jax.ShapeDtypeStruct((m, D_MODEL), BF16),
        scratch_shapes=[pltpu.VMEM((D_INNER, tn), BF16), pltpu.VMEM((ATTN_DIM, tn), BF16)],
        compiler_params=_cparams("arbitrary", "arbitrary"),
        name="branch_merge",
    )(y_ssm, y_attn, gates, gates, w_ssm_out, w_attn_out)


def _cast_kernel(w_ref, o_ref):
    o_ref[...] = w_ref[...].astype(o_ref.dtype)


def _cast_layer(w_all, layer, tr=512):
    _, k, n = w_all.shape
    return pl.pallas_call(
        _cast_kernel,
        grid=(k // tr,),
        in_specs=[pl.BlockSpec((None, tr, n), lambda i: (layer, i, 0))],
        out_specs=pl.BlockSpec((tr, n), lambda i: (i, 0)),
        out_shape=jax.ShapeDtypeStruct((k, n), BF16),
        compiler_params=_cparams("arbitrary"),
        name="weight_cast",
    )(w_all)


def _rowmm_kernel(a_ref, w_ref, x_ref, gpost_ref, gnext_ref, xo_ref, *rest, emit_next):
    ho_ref = rest[0] if emit_next else None
    kk = pl.program_id(1)
    part = jnp.dot(a_ref[...], w_ref[...], preferred_element_type=F32)

    @pl.when(kk == 0)
    def _():
        xo_ref[...] = part

    @pl.when(kk > 0)
    def _():
        xo_ref[...] += part

    @pl.when(kk == pl.num_programs(1) - 1)
    def _():
        x_new = x_ref[...] + _rms(xo_ref[...], gpost_ref[...])
        xo_ref[...] = x_new
        if emit_next:
            ho_ref[...] = _rms(x_new, gnext_ref[...]).astype(ho_ref.dtype)


def _rowmm(a, w_all, layer, x, g_post, g_next, tm=512, tk=512):
    m, k = a.shape
    w16 = _cast_layer(w_all, layer)
    d = x.shape[1]
    tm = min(tm, m)
    emit_next = g_next is not None
    if g_next is None:
        g_next = g_post
    row = pl.BlockSpec((tm, d), lambda i, kk: (i, 0))
    par = pl.BlockSpec((1, d), lambda i, kk: (0, 0))
    out_specs = [row, row] if emit_next else [row]
    out_shape = [jax.ShapeDtypeStruct((m, d), F32)] + ([jax.ShapeDtypeStruct((m, d), BF16)] if emit_next else [])
    outs = pl.pallas_call(
        functools.partial(_rowmm_kernel, emit_next=emit_next),
        grid=(m // tm, k // tk),
        in_specs=[pl.BlockSpec((tm, tk), lambda i, kk: (i, kk)),
                  pl.BlockSpec((tk, d), lambda i, kk: (kk, 0)),
                  row, par, par],
        out_specs=out_specs,
        out_shape=out_shape,
        compiler_params=_cparams("arbitrary", "arbitrary"),
        name="out_proj_norm_residual",
    )(a, w16, x, g_post.reshape(1, d), g_next.reshape(1, d))
    return (outs[0], outs[1]) if emit_next else (outs[0], None)


def _ffn_up_kernel(a_ref, wg_ref, wv_ref, cwg_ref, cwv_ref, cbg_ref, cbv_ref, o_ref,
                   wgs_ref, wvs_ref, ug_ref, uv_ref, *, tiles_per_seq):
    pad = SUBLANES
    tm = a_ref.shape[0]
    i = pl.program_id(1)

    @pl.when(i == 0)
    def _():
        _cast_weight(wg_ref, wgs_ref)
        _cast_weight(wv_ref, wvs_ref)

    @pl.when(i % tiles_per_seq == 0)
    def _():
        ug_ref[0:pad, :] = jnp.zeros((pad, ug_ref.shape[1]), F32)
        uv_ref[0:pad, :] = jnp.zeros((pad, uv_ref.shape[1]), F32)

    a = a_ref[...]

    def conv(w_s, u_ref, cw_ref, cb_ref):
        u_ref[pad:pad + tm, :] = jnp.dot(a, w_s[...], preferred_element_type=F32)
        acc = cb_ref[...] + cw_ref[FFN_CONV - 1:FFN_CONV, :] * u_ref[pad:pad + tm, :]
        for s in range(1, FFN_CONV):
            acc = acc + cw_ref[FFN_CONV - 1 - s:FFN_CONV - s, :] * u_ref[pad - s:pad - s + tm, :]
        u_ref[0:pad, :] = u_ref[tm:tm + pad, :]
        return acc

    gte = conv(wgs_ref, ug_ref, cwg_ref, cbg_ref)
    val = conv(wvs_ref, uv_ref, cwv_ref, cbv_ref)
    o_ref[...] = (jax.nn.gelu(gte, approximate=True) * val).astype(o_ref.dtype)


def _ffn_up(h, w_up, conv_w, conv_b, layer, batch, tn=512, tm=1024):
    m, k = h.shape
    seq = m // batch
    tm = min(tm, seq)
    nt = D_FF // tn
    wspec = lambda shift: pl.BlockSpec((None, k, tn), lambda j, i: (layer, 0, j + shift))
    cwspec = lambda shift: pl.BlockSpec((FFN_CONV, tn), lambda j, i: (0, j + shift))
    cbspec = lambda shift: pl.BlockSpec((1, tn), lambda j, i: (0, j + shift))
    return pl.pallas_call(
        functools.partial(_ffn_up_kernel, tiles_per_seq=seq // tm),
        grid=(nt, m // tm),
        in_specs=[pl.BlockSpec((tm, k), lambda j, i: (i, 0)),
                  wspec(0), wspec(nt), cwspec(0), cwspec(nt), cbspec(0), cbspec(nt)],
        out_specs=pl.BlockSpec((tm, tn), lambda j, i: (i, j)),
        out_shape=jax.ShapeDtypeStruct((m, D_FF), BF16),
        scratch_shapes=[pltpu.VMEM((k, tn), BF16), pltpu.VMEM((k, tn), BF16),
                        pltpu.VMEM((tm + 2 * SUBLANES, tn), F32), pltpu.VMEM((tm + 2 * SUBLANES, tn), F32)],
        compiler_params=_cparams("arbitrary", "arbitrary"),
        name="ffn_up_conv_geglu",
    )(h, w_up, w_up, conv_w, conv_w, conv_b.reshape(1, 2 * D_FF), conv_b.reshape(1, 2 * D_FF))


def kernel(x, norm_mix_pre, norm_mix_post, norm_ffn_pre, norm_ffn_post, w_in, b_gate, conv_xbc_w, conv_xbc_b, dt_bias, a_log, d_skip, ssm_norm, w_ssm_out, w_attn_out, w_mix_out, w_up, conv_ffn_w, conv_ffn_b, w_down):
    batch, seq, d = x.shape
    depth = w_in.shape[0]
    xf = x.reshape(batch * seq, d)
    h = _norm(xf, norm_mix_pre[0])
    w_t = jnp.swapaxes(w_in, 1, 2)
    for l in range(depth):
        z = _proj(h, w_t, l, OFF_Z, D_INNER, F32)
        xbc = _proj(h, w_t, l, OFF_XBC, CONV_DIM, F32)
        qkvi = _proj(h, w_t, l, OFF_Q, ATTN_DIM + 2 * KV_DIM + IDX_DIM, BF16)
        gates = _proj(h, w_t, l, OFF_GATE, N_BRANCHES * D_MODEL, F32, bias=b_gate[l])
        dt128, kidx2, widx = _smalls(h, w_t, l)
        y_ssm = _ssd(z, xbc, dt128, conv_xbc_w[l], conv_xbc_b[l], dt_bias[l], a_log[l], d_skip[l],
                     ssm_norm[l], batch)
        y_attn = _dsa(qkvi, kidx2, widx, batch)
        merged = _merge(y_ssm, y_attn, gates, w_ssm_out, w_attn_out, l)
        xf, h2 = _rowmm(merged, w_mix_out, l, xf, norm_mix_post[l], norm_ffn_pre[l])
        act = _ffn_up(h2, w_up, conv_ffn_w[l], conv_ffn_b[l], l, batch)
        xf, h = _rowmm(act, w_down, l, xf, norm_ffn_post[l], norm_mix_pre[l + 1] if l + 1 < depth else None)
    return xf.reshape(batch, seq, d)
```

```python
import functools

import numpy as np
import jax
import jax.numpy as jnp
from jax import lax
from jax.experimental import pallas as pl
from jax.experimental.pallas import tpu as pltpu

F32 = jnp.float32
BF16 = jnp.bfloat16

D_MODEL = 2048
EPS = 1e-6

D_INNER = 2 * D_MODEL
SSM_HEAD_DIM = 64
SSM_HEADS = D_INNER // SSM_HEAD_DIM
SSM_GROUPS = 8
SSM_HPG = SSM_HEADS // SSM_GROUPS
SSM_STATE = 128
SSM_CONV = 4
CONV_DIM = D_INNER + 2 * SSM_GROUPS * SSM_STATE
GROUP_W = SSM_HPG * SSM_HEAD_DIM
SSD_T = 128

CHUNK = 64
ATTN_HEAD_DIM = 128
ATTN_Q_HEADS = 16
ATTN_KV_HEADS = 4
ATTN_QPK = ATTN_Q_HEADS // ATTN_KV_HEADS
ATTN_DIM = ATTN_Q_HEADS * ATTN_HEAD_DIM
KV_DIM = ATTN_KV_HEADS * ATTN_HEAD_DIM
ATTN_SCALE = ATTN_HEAD_DIM ** -0.5
IDX_HEADS = 16
IDX_HEAD_DIM = 64
IDX_DIM = IDX_HEADS * IDX_HEAD_DIM
IDX_SCALE = (IDX_HEAD_DIM ** -0.5) * (IDX_HEADS ** -0.5)
TOPK_MAX = 256
Q_BLOCK = 128
KEY_TILE = 512

N_BRANCHES = 2
D_FF = 2 * D_MODEL
FFN_CONV = 3

IN_SPLITS = (D_INNER, CONV_DIM, SSM_HEADS, ATTN_DIM, KV_DIM, KV_DIM,
             IDX_DIM, IDX_HEAD_DIM, IDX_HEADS, N_BRANCHES * D_MODEL)
_OFFS = [0] + [int(o) for o in np.cumsum(IN_SPLITS)]
OFF_Z, OFF_XBC, OFF_DT, OFF_Q, OFF_K, OFF_V, OFF_QIDX, OFF_KIDX, OFF_WIDX, OFF_GATE, D_IN_PROJ = _OFFS

LANES = 128
SUBLANES = 8
VMEM_LIMIT = 56 * 2 ** 20

INT_MIN = -2 ** 31
NEG_BIG = -1e30


def _cparams(*sem):
    return pltpu.CompilerParams(dimension_semantics=sem, vmem_limit_bytes=VMEM_LIMIT)


def _rms(x, g):
    return x * lax.rsqrt(jnp.mean(x * x, axis=-1, keepdims=True) + EPS) * g


def _norm_kernel(x_ref, g_ref, o_ref):
    o_ref[...] = _rms(x_ref[...], g_ref[...]).astype(o_ref.dtype)


def _norm(x, g, tm=512):
    m, d = x.shape
    return pl.pallas_call(
        _norm_kernel,
        grid=(m // tm,),
        in_specs=[pl.BlockSpec((tm, d), lambda i: (i, 0)),
                  pl.BlockSpec((1, d), lambda i: (0, 0))],
        out_specs=pl.BlockSpec((tm, d), lambda i: (i, 0)),
        out_shape=jax.ShapeDtypeStruct((m, d), BF16),
        compiler_params=_cparams("arbitrary"),
        name="rmsnorm",
    )(x, g.reshape(1, d))


def _cast_weight(w_ref, wsc_ref, kc=256):
    def body(i, c):
        rows = pl.ds(pl.multiple_of(i * kc, kc), kc)
        wsc_ref[rows, :] = w_ref[rows, :].astype(BF16)
        return c

    lax.fori_loop(0, wsc_ref.shape[0] // kc, body, 0)


def _dot_nt(a, b):
    return lax.dot_general(a, b, (((1,), (1,)), ((), ())), preferred_element_type=F32)


def _proj_kernel(*refs, gate, tm):
    refs = list(refs)
    h_ref = refs.pop(0)
    w_ref = refs.pop(0)
    b_ref = refs.pop(0) if gate else None
    o_ref, wsc_ref = refs
    i = pl.program_id(1)

    @pl.when(i == 0)
    def _():
        wsc_ref[...] = w_ref[0].astype(BF16)

    a = h_ref[pl.ds(pl.multiple_of(i * tm, tm), tm), :]
    acc = _dot_nt(a, wsc_ref[...])
    if gate:
        acc = jax.nn.sigmoid(acc + b_ref[...])
    o_ref[...] = acc.astype(o_ref.dtype)


def _proj(h, w_t, layer, row_start, n_rows, out_dtype, bias=None, tn=512, tm=2048):
    m, k = h.shape
    tm = min(tm, m)
    nt = n_rows // tn
    assert row_start % SUBLANES == 0 and n_rows % tn == 0
    in_specs = [pl.BlockSpec((m, k), lambda j, i: (0, 0), pipeline_mode=pl.Buffered(1)),
                pl.BlockSpec((pl.Element(1), pl.Element(tn), pl.Element(k)),
                             lambda j, i: (layer, pl.multiple_of(row_start + j * tn, SUBLANES), 0))]
    args = [h, w_t]
    if bias is not None:
        in_specs.append(pl.BlockSpec((1, tn), lambda j, i: (0, j)))
        args.append(bias.reshape(1, n_rows))
    return pl.pallas_call(
        functools.partial(_proj_kernel, gate=bias is not None, tm=tm),
        grid=(nt, m // tm),
        in_specs=in_specs,
        out_specs=pl.BlockSpec((tm, tn), lambda j, i: (i, j)),
        out_shape=jax.ShapeDtypeStruct((m, n_rows), out_dtype),
        scratch_shapes=[pltpu.VMEM((tn, k), BF16)],
        compiler_params=_cparams("arbitrary", "arbitrary"),
        name="in_proj",
    )(*args)


def _smalls_kernel(a_ref, wdt_ref, wk_ref, ww_ref, dt_ref, kidx_ref, widx_ref):
    a = a_ref[...]
    k = a.shape[1]

    def pad_rows(w):
        return jnp.concatenate([w, jnp.zeros((LANES - w.shape[0], k), BF16)], axis=0)

    wk = wk_ref[...].astype(BF16)
    dt_ref[...] = _dot_nt(a, pad_rows(wdt_ref[...].astype(BF16)))
    widx_ref[...] = _dot_nt(a, pad_rows(ww_ref[...].astype(BF16)))
    kidx_ref[...] = _dot_nt(a, jnp.concatenate([wk, wk], axis=0)).astype(kidx_ref.dtype)


def _smalls(h, w_t, layer, tm=1024):
    m, k = h.shape
    tm = min(tm, m)
    assert 2 * IDX_HEAD_DIM == LANES and IDX_HEADS % 16 == 0
    wspec = lambda start, n: pl.BlockSpec((None, n, k), lambda i: (layer, start // n, 0))
    ospec = pl.BlockSpec((tm, LANES), lambda i: (i, 0))
    return pl.pallas_call(
        _smalls_kernel,
        grid=(m // tm,),
        in_specs=[pl.BlockSpec((tm, k), lambda i: (i, 0)),
                  wspec(OFF_DT, SSM_HEADS), wspec(OFF_KIDX, IDX_HEAD_DIM), wspec(OFF_WIDX, IDX_HEADS)],
        out_specs=[ospec, ospec, ospec],
        out_shape=[jax.ShapeDtypeStruct((m, LANES), F32),
                   jax.ShapeDtypeStruct((m, LANES), BF16),
                   jax.ShapeDtypeStruct((m, LANES), F32)],
        compiler_params=_cparams("arbitrary"),
        name="in_proj_narrow",
    )(h, w_t, w_t, w_t)


def _silu(x):
    h = 0.5 * x
    return h + h * jnp.tanh(h)


def _ssd_kernel(z_ref, xbc_ref, dt_ref, cw_ref, cb_ref, dtb_ref, alog_ref, dskip_ref, ng_ref,
                o_ref, xbuf_ref, xc_ref, st_ref, w8_ref):
    t = SSD_T
    pad = SUBLANES

    @pl.when(pl.program_id(1) == 0)
    def _():
        xbuf_ref[0:pad, :] = jnp.zeros((pad, CONV_DIM), F32)
        st_ref[...] = jnp.zeros_like(st_ref)
        for i in range(SSM_CONV):
            w8_ref[i] = jnp.broadcast_to(cw_ref[i:i + 1, :], (pad, CONV_DIM))
        w8_ref[SSM_CONV] = jnp.broadcast_to(cb_ref[...], (pad, CONV_DIM))

    xbuf_ref[pad:pad + t, :] = xbc_ref[...].astype(F32)
    cw = 512
    for c in range(CONV_DIM // cw):
        sl = slice(c * cw, (c + 1) * cw)
        acc = w8_ref[SSM_CONV, :, sl][None]
        for i in range(SSM_CONV):
            shifted = xbuf_ref[pad - i:pad - i + t, sl].reshape(t // pad, pad, cw)
            acc = acc + w8_ref[SSM_CONV - 1 - i, :, sl][None] * shifted
        xc_ref[:, sl] = _silu(acc).reshape(t, cw)
    xbuf_ref[0:pad, :] = xbuf_ref[t:t + pad, :]

    lane = lax.broadcasted_iota(jnp.int32, (t, LANES), 1)
    row = lax.broadcasted_iota(jnp.int32, (t, LANES), 0)
    head_ok = lane < SSM_HEADS
    x_dt = dt_ref[...] + dtb_ref[...]
    dt = jnp.maximum(x_dt, 0.0) + jnp.log1p(jnp.exp(-jnp.abs(x_dt)))
    dt = jnp.where(head_ok, dt, 0.0)
    a_dt = dt * (-jnp.exp(alog_ref[...]))
    tril = row >= lane
    cs = jnp.dot(tril.astype(F32), a_dt, preferred_element_type=F32,
                 precision=lax.Precision.HIGHEST)
    cs_t = cs.T
    dt_t = dt.T
    cs_last = cs[t - 1:t, :]
    dte = jnp.exp(cs_last - cs)
    expand_rows = jnp.concatenate(
        [dt * dte, jnp.exp(cs), jnp.broadcast_to(jnp.exp(cs_last), (SUBLANES, LANES))], axis=0)
    hi = expand_rows.astype(BF16)
    rem = expand_rows - hi.astype(F32)
    mid = rem.astype(BF16)
    lo = (rem - mid.astype(F32)).astype(BF16)
    expand_rows3 = jnp.concatenate([hi, mid, lo], axis=1)

    gh = lax.broadcasted_iota(jnp.int32, (LANES, GROUP_W), 0)
    gc = lax.broadcasted_iota(jnp.int32, (LANES, GROUP_W), 1) >> (SSM_HEAD_DIM.bit_length() - 1)
    half = lax.broadcasted_iota(jnp.int32, (t, 2 * SSM_HEAD_DIM), 1) < SSM_HEAD_DIM

    for g in range(SSM_GROUPS):
        sl = slice(g * GROUP_W, (g + 1) * GROUP_W)
        expand = jnp.where(gh == gc + g * SSM_HPG, 1.0, 0.0).astype(BF16)
        ex = jnp.dot(expand_rows3, jnp.concatenate([expand, expand, expand], axis=0),
                     preferred_element_type=F32)
        dtdte_e, ecs_e, cd_e = ex[0:t], ex[t:2 * t], ex[2 * t:2 * t + 1]
        xs = xc_ref[:, sl]
        bm = xc_ref[:, D_INNER + g * SSM_STATE:D_INNER + (g + 1) * SSM_STATE]
        cm = xc_ref[:, D_INNER + (SSM_GROUPS + g) * SSM_STATE:D_INNER + (SSM_GROUPS + g + 1) * SSM_STATE]
        bm16, cm16 = bm.astype(BF16), cm.astype(BF16)
        xs16 = xs.astype(BF16)
        cb = _dot_nt(cm16, bm16)
        h_in = st_ref[:, sl]
        y = jnp.dot(cm16, h_in.astype(BF16), preferred_element_type=F32) * ecs_e
        new_states = jnp.dot(bm.T.astype(BF16), (xs * dtdte_e).astype(BF16), preferred_element_type=F32)
        st_ref[:, sl] = h_in * cd_e + new_states
        yd = []
        for j in range(SSM_HPG // 2):
            h1 = g * SSM_HPG + 2 * j
            ms = []
            for hh in (h1, h1 + 1):
                dec = jnp.exp(jnp.where(tril, cs[:, hh:hh + 1] - cs_t[hh:hh + 1, :], -jnp.inf))
                ms.append((cb * (dec * dt_t[hh:hh + 1, :])).astype(BF16))
            xp = xs16[:, 2 * j * SSM_HEAD_DIM:(2 * j + 2) * SSM_HEAD_DIM]
            zero = jnp.zeros_like(xp)
            blockdiag = jnp.concatenate([jnp.where(half, xp, zero), jnp.where(half, zero, xp)], axis=0)
            yd.append(jnp.dot(jnp.concatenate(ms, axis=1), blockdiag, preferred_element_type=F32))
        y = y + jnp.concatenate(yd, axis=1) + dskip_ref[:, sl] * xs
        y = y * _silu(z_ref[:, sl].astype(F32))
        y = y * lax.rsqrt(jnp.mean(y * y, axis=-1, keepdims=True) + EPS)
        o_ref[:, sl] = (y * ng_ref[:, sl]).astype(o_ref.dtype)


def _ssd(z, xbc, dt128, conv_w, conv_b, dt_bias, a_log, d_skip, norm_g, batch):
    m = z.shape[0]
    nblk = m // batch // SSD_T
    pad_heads = lambda v: jnp.pad(v, (0, LANES - SSM_HEADS)).reshape(1, LANES)
    row_spec = lambda w: pl.BlockSpec((SSD_T, w), lambda b, i: (b * nblk + i, 0))
    par_spec = lambda r, w: pl.BlockSpec((r, w), lambda b, i: (0, 0))
    return pl.pallas_call(
        _ssd_kernel,
        grid=(batch, nblk),
        in_specs=[row_spec(D_INNER), row_spec(CONV_DIM), row_spec(LANES),
                  par_spec(SSM_CONV, CONV_DIM), par_spec(1, CONV_DIM),
                  par_spec(1, LANES), par_spec(1, LANES), par_spec(1, D_INNER), par_spec(1, D_INNER)],
        out_specs=row_spec(D_INNER),
        out_shape=jax.ShapeDtypeStruct((m, D_INNER), BF16),
        scratch_shapes=[pltpu.VMEM((SSD_T + 2 * SUBLANES, CONV_DIM), F32),
                        pltpu.VMEM((SSD_T, CONV_DIM), F32),
                        pltpu.VMEM((SSM_STATE, D_INNER), F32),
                        pltpu.VMEM((SSM_CONV + 1, SUBLANES, CONV_DIM), F32)],
        compiler_params=_cparams("arbitrary", "arbitrary"),
        name="ssd_mixer",
    )(z, xbc, dt128, conv_w, conv_b.reshape(1, CONV_DIM), pad_heads(dt_bias), pad_heads(a_log),
      jnp.repeat(d_skip, SSM_HEAD_DIM).reshape(1, D_INNER), norm_g.reshape(1, D_INNER))


def _dsa_kernel(q_ref, qi_ref, w_ref, k_ref, v_ref, ki_ref, o_ref,
                keys_ref, bias_ref, qs_ref, qh_ref, *, top_k, seq):
    qb, kt = Q_BLOCK, KEY_TILE
    blk = pl.program_id(1)
    ntiles = (blk * qb + qb + kt - 1) // kt
    lane_q = lax.broadcasted_iota(jnp.int32, (1, qb), 1)
    limit = (((blk * qb + lane_q) >> (CHUNK.bit_length() - 1)) + 1) * CHUNK
    pos_t = lax.broadcasted_iota(jnp.int32, (kt, qb), 0)
    fold_rows = 64

    def count(pred):
        def body(t, acc):
            x = jnp.where(pred(t), 1.0, 0.0)
            return acc + jnp.sum(x.reshape(kt // fold_rows, fold_rows, qb), axis=0)
        acc = lax.fori_loop(0, ntiles, body, jnp.zeros((fold_rows, qb), F32))
        return jnp.sum(acc, axis=0, keepdims=True)

    w_t = w_ref[...].T
    lane_d = lax.broadcasted_iota(jnp.int32, (qb, LANES), 1)
    for p in range(IDX_HEADS // 2):
        qp = qi_ref[:, p * LANES:(p + 1) * LANES]
        zero = jnp.zeros_like(qp)
        qh_ref[p, 0:qb, :] = jnp.where(lane_d < IDX_HEAD_DIM, qp, zero)
        qh_ref[p, qb:2 * qb, :] = jnp.where(lane_d < IDX_HEAD_DIM, zero, qp)

    def score_tile(t, c):
        kx = ki_ref[pl.ds(pl.multiple_of(t * kt, kt), kt), :]
        acc = jnp.zeros((kt, qb), F32)
        for p in range(IDX_HEADS // 2):
            s2 = _dot_nt(kx, qh_ref[p])
            for e in range(2):
                hd = 2 * p + e
                acc = acc + jnp.maximum(s2[:, e * qb:(e + 1) * qb], 0.0) * w_t[hd:hd + 1, :]
        score = acc * IDX_SCALE + 0.0
        bits = pltpu.bitcast(score, jnp.int32)
        key = jnp.where(bits < 0, bits ^ jnp.int32(0x7FFFFFFF), bits)
        keys_ref[t] = jnp.where(t * kt + pos_t < limit, key, jnp.int32(INT_MIN))
        return c

    lax.fori_loop(0, ntiles, score_tile, 0)

    def bit_step(it, tb):
        cand = tb | lax.shift_left(jnp.int32(1), jnp.int32(31) - it)
        cand_s = cand ^ jnp.int32(INT_MIN)
        cnt = count(lambda t: keys_ref[t] >= cand_s)
        return jnp.where(cnt >= float(top_k), cand, tb)

    tb = lax.fori_loop(0, 32, bit_step, jnp.zeros((1, qb), jnp.int32))
    thr = tb ^ jnp.int32(INT_MIN)

    cnt_gt = count(lambda t: keys_ref[t] > thr)
    cnt_ge = cnt_gt + count(lambda t: keys_ref[t] == thr)
    excess = (cnt_ge > float(top_k)) & (thr != jnp.int32(INT_MIN))
    need = float(top_k) - cnt_gt

    def tie_cut():
        def step(it, mp):
            cand = mp | lax.shift_left(jnp.int32(1), jnp.int32(seq.bit_length() - 1) - it)
            cnt = count(lambda t: (keys_ref[t] == thr) & (t * kt + pos_t < cand))
            return jnp.where(cnt < need, cand, mp)
        return lax.fori_loop(0, seq.bit_length(), step, jnp.zeros((1, qb), jnp.int32))

    any_excess = jnp.max(jnp.where(excess, 1.0, 0.0)) > 0.0
    last_tie = lax.cond(any_excess, tie_cut, lambda: jnp.full((1, qb), seq, jnp.int32))

    def bias_tile(t, c):
        key = keys_ref[t]
        pos = t * kt + pos_t
        sel = (key > thr) | ((key == thr) & (pos <= last_tie))
        b_t = jnp.where(sel & (pos < limit), 0.0, NEG_BIG)
        for c4 in range(kt // LANES):
            bias_ref[t, :, c4 * LANES:(c4 + 1) * LANES] = b_t[c4 * LANES:(c4 + 1) * LANES, :].T
        return c

    lax.fori_loop(0, ntiles, bias_tile, 0)

    log2_scale = ATTN_SCALE * float(np.log2(np.e))
    for g in range(ATTN_KV_HEADS):
        for j in range(ATTN_QPK):
            hq = g * ATTN_QPK + j
            qs_ref[j * qb:(j + 1) * qb, :] = q_ref[:, hq * ATTN_HEAD_DIM:(hq + 1) * ATTN_HEAD_DIM]
        n_chain = 1
        hpc = ATTN_QPK // n_chain
        rows = hpc * qb

        def att_tile(t, carry):
            ks = pl.ds(pl.multiple_of(t * kt, kt), kt)
            kx = k_ref[ks, g * ATTN_HEAD_DIM:(g + 1) * ATTN_HEAD_DIM]
            vx = v_ref[ks, g * ATTN_HEAD_DIM:(g + 1) * ATTN_HEAD_DIM]
            bias = bias_ref[t][None]
            new = []
            for c in range(n_chain):
                m_i, l_i, acc = carry[c]
                s = _dot_nt(qs_ref[c * rows:(c + 1) * rows, :], kx)
                s = ((s * log2_scale).reshape(hpc, qb, kt) + bias).reshape(rows, kt)
                m_new = jnp.maximum(m_i, jnp.max(s, axis=1, keepdims=True))
                p = jnp.exp2(s - m_new)
                alpha = jnp.exp2(m_i - m_new)
                new.append((m_new, alpha * l_i + jnp.sum(p, axis=1, keepdims=True),
                            alpha * acc + jnp.dot(p.astype(BF16), vx, preferred_element_type=F32)))
            return tuple(new)

        init = tuple((jnp.full((rows, 1), NEG_BIG, F32), jnp.zeros((rows, 1), F32),
                      jnp.zeros((rows, ATTN_HEAD_DIM), F32)) for _ in range(n_chain))
        res = lax.fori_loop(0, ntiles, att_tile, init)
        for c in range(n_chain):
            out = res[c][2] / res[c][1]
            for j in range(hpc):
                hq = g * ATTN_QPK + c * hpc + j
                o_ref[:, hq * ATTN_HEAD_DIM:(hq + 1) * ATTN_HEAD_DIM] = out[j * qb:(j + 1) * qb].astype(o_ref.dtype)


def _dsa(qkvi, kidx2, widx, batch):
    m = qkvi.shape[0]
    seq = m // batch
    nb = seq // Q_BLOCK
    top_k = min(TOPK_MAX, seq // 4)
    c_k, c_v, c_qi = ATTN_DIM // KV_DIM, ATTN_DIM // KV_DIM + 1, (ATTN_DIM + 2 * KV_DIM) // IDX_DIM
    return pl.pallas_call(
        functools.partial(_dsa_kernel, top_k=top_k, seq=seq),
        grid=(batch, nb),
        in_specs=[pl.BlockSpec((Q_BLOCK, ATTN_DIM), lambda b, i: (b * nb + i, 0)),
                  pl.BlockSpec((Q_BLOCK, IDX_DIM), lambda b, i: (b * nb + i, c_qi)),
                  pl.BlockSpec((Q_BLOCK, LANES), lambda b, i: (b * nb + i, 0)),
                  pl.BlockSpec((seq, KV_DIM), lambda b, i: (b, c_k)),
                  pl.BlockSpec((seq, KV_DIM), lambda b, i: (b, c_v)),
                  pl.BlockSpec((seq, LANES), lambda b, i: (b, 0))],
        out_specs=pl.BlockSpec((Q_BLOCK, ATTN_DIM), lambda b, i: (b * nb + i, 0)),
        out_shape=jax.ShapeDtypeStruct((m, ATTN_DIM), BF16),
        scratch_shapes=[pltpu.VMEM((seq // KEY_TILE, KEY_TILE, Q_BLOCK), jnp.int32),
                        pltpu.VMEM((seq // KEY_TILE, Q_BLOCK, KEY_TILE), F32),
                        pltpu.VMEM((ATTN_QPK * Q_BLOCK, ATTN_HEAD_DIM), BF16),
                        pltpu.VMEM((IDX_HEADS // 2, 2 * Q_BLOCK, LANES), BF16)],
        compiler_params=_cparams("arbitrary", "arbitrary"),
        name="dsa_mixer",
    )(qkvi, qkvi, widx, qkvi, qkvi, kidx2)


def _merge_kernel(ys_ref, ya_ref, gs_ref, ga_ref, ws_ref, wa_ref, o_ref, wss_ref, was_ref):
    @pl.when(pl.program_id(1) == 0)
    def _():
        _cast_weight(ws_ref, wss_ref)
        _cast_weight(wa_ref, was_ref)

    ps = jnp.dot(ys_ref[...], wss_ref[...], preferred_element_type=F32)
    pa = jnp.dot(ya_ref[...], was_ref[...], preferred_element_type=F32)
    o_ref[...] = (gs_ref[...].astype(F32) * ps + ga_ref[...].astype(F32) * pa).astype(o_ref.dtype)


def _merge(y_ssm, y_attn, gates, w_ssm_out, w_attn_out, layer, tn=512, tm=512):
    m = y_ssm.shape[0]
    tm = min(tm, m)
    nt = D_MODEL // tn
    return pl.pallas_call(
        _merge_kernel,
        grid=(nt, m // tm),
        in_specs=[pl.BlockSpec((tm, D_INNER), lambda j, i: (i, 0)),
                  pl.BlockSpec((tm, ATTN_DIM), lambda j, i: (i, 0)),
                  pl.BlockSpec((tm, tn), lambda j, i: (i, j)),
                  pl.BlockSpec((tm, tn), lambda j, i: (i, nt + j)),
                  pl.BlockSpec((None, D_INNER, tn), lambda j, i: (layer, 0, j)),
                  pl.BlockSpec((None, ATTN_DIM, tn), lambda j, i: (layer, 0, j))],
        out_specs=pl.BlockSpec((tm, tn), lambda j, i: (i, j)),
        out_shape=jax.ShapeDtypeStruct((m, D_MODEL), BF16),
        scratch_shapes=[pltpu.VMEM((D_INNER, tn), BF16), pltpu.VMEM((ATTN_DIM, tn), BF16)],
        compiler_params=_cparams("arbitrary", "arbitrary"),
        name="branch_merge",
    )(y_ssm, y_attn, gates, gates, w_ssm_out, w_attn_out)


def _cast_kernel(w_ref, o_ref):
    o_ref[...] = w_ref[...].astype(o_ref.dtype)


def _cast_layer(w_all, layer, tr=512):
    _, k, n = w_all.shape
    return pl.pallas_call(
        _cast_kernel,
        grid=(k // tr,),
        in_specs=[pl.BlockSpec((None, tr, n), lambda i: (layer, i, 0))],
        out_specs=pl.BlockSpec((tr, n), lambda i: (i, 0)),
        out_shape=jax.ShapeDtypeStruct((k, n), BF16),
        compiler_params=_cparams("arbitrary"),
        name="weight_cast",
    )(w_all)


def _rowmm_kernel(a_ref, w_ref, x_ref, gpost_ref, gnext_ref, xo_ref, *rest, emit_next):
    ho_ref = rest[0] if emit_next else None
    y = jnp.dot(a_ref[...], w_ref[...], preferred_element_type=F32)
    x_new = x_ref[...] + _rms(y, gpost_ref[...])
    xo_ref[...] = x_new
    if emit_next:
        ho_ref[...] = _rms(x_new, gnext_ref[...]).astype(ho_ref.dtype)


def _rowmm(a, w_all, layer, x, g_post, g_next, tm=512):
    m, k = a.shape
    w16 = _cast_layer(w_all, layer)
    d = x.shape[1]
    tm = min(tm, m)
    emit_next = g_next is not None
    if g_next is None:
        g_next = g_post
    row = pl.BlockSpec((tm, d), lambda i: (i, 0))
    par = pl.BlockSpec((1, d), lambda i: (0, 0))
    out_specs = [row, row] if emit_next else [row]
    out_shape = [jax.ShapeDtypeStruct((m, d), F32)] + ([jax.ShapeDtypeStruct((m, d), BF16)] if emit_next else [])
    outs = pl.pallas_call(
        functools.partial(_rowmm_kernel, emit_next=emit_next),
        grid=(m // tm,),
        in_specs=[pl.BlockSpec((tm, k), lambda i: (i, 0)),
                  pl.BlockSpec((k, d), lambda i: (0, 0), pipeline_mode=pl.Buffered(1)),
                  row, par, par],
        out_specs=out_specs,
        out_shape=out_shape,
        compiler_params=_cparams("arbitrary"),
        name="out_proj_norm_residual",
    )(a, w16, x, g_post.reshape(1, d), g_next.reshape(1, d))
    return (outs[0], outs[1]) if emit_next else (outs[0], None)


def _ffn_up_kernel(a_ref, wg_ref, wv_ref, cwg_ref, cwv_ref, cbg_ref, cbv_ref, o_ref,
                   wgs_ref, wvs_ref, ug_ref, uv_ref, *, tiles_per_seq):
    pad = SUBLANES
    tm = a_ref.shape[0]
    i = pl.program_id(1)

    @pl.when(i == 0)
    def _():
        _cast_weight(wg_ref, wgs_ref)
        _cast_weight(wv_ref, wvs_ref)

    @pl.when(i % tiles_per_seq == 0)
    def _():
        ug_ref[0:pad, :] = jnp.zeros((pad, ug_ref.shape[1]), F32)
        uv_ref[0:pad, :] = jnp.zeros((pad, uv_ref.shape[1]), F32)

    a = a_ref[...]

    def conv(w_s, u_ref, cw_ref, cb_ref):
        u_ref[pad:pad + tm, :] = jnp.dot(a, w_s[...], preferred_element_type=F32)
        acc = cb_ref[...] + cw_ref[FFN_CONV - 1:FFN_CONV, :] * u_ref[pad:pad + tm, :]
        for s in range(1, FFN_CONV):
            acc = acc + cw_ref[FFN_CONV - 1 - s:FFN_CONV - s, :] * u_ref[pad - s:pad - s + tm, :]
        u_ref[0:pad, :] = u_ref[tm:tm + pad, :]
        return acc

    gte = conv(wgs_ref, ug_ref, cwg_ref, cbg_ref)
    val = conv(wvs_ref, uv_ref, cwv_ref, cbv_ref)
    o_ref[...] = (jax.nn.gelu(gte, approximate=True) * val).astype(o_ref.dtype)


def _ffn_up(h, w_up, conv_w, conv_b, layer, batch, tn=512, tm=1024):
    m, k = h.shape
    seq = m // batch
    tm = min(tm, seq)
    nt = D_FF // tn
    wspec = lambda shift: pl.BlockSpec((None, k, tn), lambda j, i: (layer, 0, j + shift))
    cwspec = lambda shift: pl.BlockSpec((FFN_CONV, tn), lambda j, i: (0, j + shift))
    cbspec = lambda shift: pl.BlockSpec((1, tn), lambda j, i: (0, j + shift))
    return pl.pallas_call(
        functools.partial(_ffn_up_kernel, tiles_per_seq=seq // tm),
        grid=(nt, m // tm),
        in_specs=[pl.BlockSpec((tm, k), lambda j, i: (i, 0)),
                  wspec(0), wspec(nt), cwspec(0), cwspec(nt), cbspec(0), cbspec(nt)],
        out_specs=pl.BlockSpec((tm, tn), lambda j, i: (i, j)),
        out_shape=jax.ShapeDtypeStruct((m, D_FF), BF16),
        scratch_shapes=[pltpu.VMEM((k, tn), BF16), pltpu.VMEM((k, tn), BF16),
                        pltpu.VMEM((tm + 2 * SUBLANES, tn), F32), pltpu.VMEM((tm + 2 * SUBLANES, tn), F32)],
        compiler_params=_cparams("arbitrary", "arbitrary"),
        name="ffn_up_conv_geglu",
    )(h, w_up, w_up, conv_w, conv_w, conv_b.reshape(1, 2 * D_FF), conv_b.reshape(1, 2 * D_FF))


def kernel(x, norm_mix_pre, norm_mix_post, norm_ffn_pre, norm_ffn_post, w_in, b_gate, conv_xbc_w, conv_xbc_b, dt_bias, a_log, d_skip, ssm_norm, w_ssm_out, w_attn_out, w_mix_out, w_up, conv_ffn_w, conv_ffn_b, w_down):
    batch, seq, d = x.shape
    depth = w_in.shape[0]
    xf = x.reshape(batch * seq, d)
    h = _norm(xf, norm_mix_pre[0])
    w_t = jnp.swapaxes(w_in, 1, 2)
    for l in range(depth):
        z = _proj(h, w_t, l, OFF_Z, D_INNER, BF16)
        xbc = _proj(h, w_t, l, OFF_XBC, CONV_DIM, BF16)
        qkvi = _proj(h, w_t, l, OFF_Q, ATTN_DIM + 2 * KV_DIM + IDX_DIM, BF16)
        gates = _proj(h, w_t, l, OFF_GATE, N_BRANCHES * D_MODEL, BF16, bias=b_gate[l])
        dt128, kidx2, widx = _smalls(h, w_t, l)
        y_ssm = _ssd(z, xbc, dt128, conv_xbc_w[l], conv_xbc_b[l], dt_bias[l], a_log[l], d_skip[l],
                     ssm_norm[l], batch)
        y_attn = _dsa(qkvi, kidx2, widx, batch)
        merged = _merge(y_ssm, y_attn, gates, w_ssm_out, w_attn_out, l)
        xf, h2 = _rowmm(merged, w_mix_out, l, xf, norm_mix_post[l], norm_ffn_pre[l])
        act = _ffn_up(h2, w_up, conv_ffn_w[l], conv_ffn_b[l], l, batch)
        xf, h = _rowmm(act, w_down, l, xf, norm_ffn_post[l], norm_mix_pre[l + 1] if l + 1 < depth else None)
    return xf.reshape(batch, seq, d)
```

```python
import functools

import numpy as np
import jax
import jax.numpy as jnp
from jax import lax
from jax.experimental import pallas as pl
from jax.experimental.pallas import tpu as pltpu

F32 = jnp.float32
BF16 = jnp.bfloat16

D_MODEL = 2048
EPS = 1e-6

D_INNER = 2 * D_MODEL
SSM_HEAD_DIM = 64
SSM_HEADS = D_INNER // SSM_HEAD_DIM
SSM_GROUPS = 8
SSM_HPG = SSM_HEADS // SSM_GROUPS
SSM_STATE = 128
SSM_CONV = 4
CONV_DIM = D_INNER + 2 * SSM_GROUPS * SSM_STATE
GROUP_W = SSM_HPG * SSM_HEAD_DIM
SSD_T = 128

CHUNK = 64
ATTN_HEAD_DIM = 128
ATTN_Q_HEADS = 16
ATTN_KV_HEADS = 4
ATTN_QPK = ATTN_Q_HEADS // ATTN_KV_HEADS
ATTN_DIM = ATTN_Q_HEADS * ATTN_HEAD_DIM
KV_DIM = ATTN_KV_HEADS * ATTN_HEAD_DIM
ATTN_SCALE = ATTN_HEAD_DIM ** -0.5
IDX_HEADS = 16
IDX_HEAD_DIM = 64
IDX_DIM = IDX_HEADS * IDX_HEAD_DIM
IDX_SCALE = (IDX_HEAD_DIM ** -0.5) * (IDX_HEADS ** -0.5)
TOPK_MAX = 256
Q_BLOCK = 128
KEY_TILE = 512

N_BRANCHES = 2
D_FF = 2 * D_MODEL
FFN_CONV = 3

IN_SPLITS = (D_INNER, CONV_DIM, SSM_HEADS, ATTN_DIM, KV_DIM, KV_DIM,
             IDX_DIM, IDX_HEAD_DIM, IDX_HEADS, N_BRANCHES * D_MODEL)
_OFFS = [0] + [int(o) for o in np.cumsum(IN_SPLITS)]
OFF_Z, OFF_XBC, OFF_DT, OFF_Q, OFF_K, OFF_V, OFF_QIDX, OFF_KIDX, OFF_WIDX, OFF_GATE, D_IN_PROJ = _OFFS

LANES = 128
SUBLANES = 8
VMEM_LIMIT = 56 * 2 ** 20

INT_MIN = -2 ** 31
NEG_BIG = -1e30


def _cparams(*sem):
    return pltpu.CompilerParams(dimension_semantics=sem, vmem_limit_bytes=VMEM_LIMIT)


def _rms(x, g):
    return x * lax.rsqrt(jnp.mean(x * x, axis=-1, keepdims=True) + EPS) * g


def _norm_kernel(x_ref, g_ref, o_ref):
    o_ref[...] = _rms(x_ref[...], g_ref[...]).astype(o_ref.dtype)


def _norm(x, g, tm=512):
    m, d = x.shape
    return pl.pallas_call(
        _norm_kernel,
        grid=(m // tm,),
        in_specs=[pl.BlockSpec((tm, d), lambda i: (i, 0)),
                  pl.BlockSpec((1, d), lambda i: (0, 0))],
        out_specs=pl.BlockSpec((tm, d), lambda i: (i, 0)),
        out_shape=jax.ShapeDtypeStruct((m, d), BF16),
        compiler_params=_cparams("arbitrary"),
        name="rmsnorm",
    )(x, g.reshape(1, d))


def _cast_weight(w_ref, wsc_ref, kc=256):
    def body(i, c):
        rows = pl.ds(pl.multiple_of(i * kc, kc), kc)
        wsc_ref[rows, :] = w_ref[rows, :].astype(BF16)
        return c

    lax.fori_loop(0, wsc_ref.shape[0] // kc, body, 0)


def _dot_nt(a, b):
    return lax.dot_general(a, b, (((1,), (1,)), ((), ())), preferred_element_type=F32)


def _proj_kernel(*refs, gate, tm):
    refs = list(refs)
    h_ref = refs.pop(0)
    w_ref = refs.pop(0)
    b_ref = refs.pop(0) if gate else None
    o_ref, wsc_ref = refs
    i = pl.program_id(1)

    @pl.when(i == 0)
    def _():
        wsc_ref[...] = w_ref[0].astype(BF16)

    a = h_ref[pl.ds(pl.multiple_of(i * tm, tm), tm), :]
    acc = _dot_nt(a, wsc_ref[...])
    if gate:
        acc = jax.nn.sigmoid(acc + b_ref[...])
    o_ref[...] = acc.astype(o_ref.dtype)


def _proj(h, w_t, layer, row_start, n_rows, out_dtype, bias=None, tn=512, tm=2048):
    m, k = h.shape
    tm = min(tm, m)
    nt = n_rows // tn
    assert row_start % SUBLANES == 0 and n_rows % tn == 0
    in_specs = [pl.BlockSpec((m, k), lambda j, i: (0, 0), pipeline_mode=pl.Buffered(1)),
                pl.BlockSpec((pl.Element(1), pl.Element(tn), pl.Element(k)),
                             lambda j, i: (layer, pl.multiple_of(row_start + j * tn, SUBLANES), 0))]
    args = [h, w_t]
    if bias is not None:
        in_specs.append(pl.BlockSpec((1, tn), lambda j, i: (0, j)))
        args.append(bias.reshape(1, n_rows))
    return pl.pallas_call(
        functools.partial(_proj_kernel, gate=bias is not None, tm=tm),
        grid=(nt, m // tm),
        in_specs=in_specs,
        out_specs=pl.BlockSpec((tm, tn), lambda j, i: (i, j)),
        out_shape=jax.ShapeDtypeStruct((m, n_rows), out_dtype),
        scratch_shapes=[pltpu.VMEM((tn, k), BF16)],
        compiler_params=_cparams("arbitrary", "arbitrary"),
        name="in_proj",
    )(*args)


def _silu(x):
    h = 0.5 * x
    return h + h * jnp.tanh(h)


def _smalls_kernel(a_ref, wdt_ref, wk_ref, ww_ref, dt_ref, kidx_ref, widx_ref):
    a = a_ref[...]
    k = a.shape[1]

    def pad_rows(w):
        return jnp.concatenate([w, jnp.zeros((LANES - w.shape[0], k), BF16)], axis=0)

    wk = wk_ref[...].astype(BF16)
    dt_ref[...] = _dot_nt(a, pad_rows(wdt_ref[...].astype(BF16)))
    widx_ref[...] = _dot_nt(a, pad_rows(ww_ref[...].astype(BF16)))
    kidx_ref[...] = _dot_nt(a, jnp.concatenate([wk, wk], axis=0)).astype(kidx_ref.dtype)


def _smalls(h, w_t, layer, tm=1024):
    m, k = h.shape
    tm = min(tm, m)
    assert 2 * IDX_HEAD_DIM == LANES and IDX_HEADS % 16 == 0
    wspec = lambda start, n: pl.BlockSpec((None, n, k), lambda i: (layer, start // n, 0))
    ospec = pl.BlockSpec((tm, LANES), lambda i: (i, 0))
    return pl.pallas_call(
        _smalls_kernel,
        grid=(m // tm,),
        in_specs=[pl.BlockSpec((tm, k), lambda i: (i, 0)),
                  wspec(OFF_DT, SSM_HEADS), wspec(OFF_KIDX, IDX_HEAD_DIM), wspec(OFF_WIDX, IDX_HEADS)],
        out_specs=[ospec, ospec, ospec],
        out_shape=[jax.ShapeDtypeStruct((m, LANES), F32),
                   jax.ShapeDtypeStruct((m, LANES), BF16),
                   jax.ShapeDtypeStruct((m, LANES), F32)],
        compiler_params=_cparams("arbitrary"),
        name="in_proj_narrow",
    )(h, w_t, w_t, w_t)


def _ssd_kernel(z_ref, xbc_ref, dt_ref, cw_ref, cb_ref, dtb_ref, alog_ref, dskip_ref, ng_ref,
                o_ref, xb_ref, xc_ref, st_ref, w8_ref):
    t = SSD_T
    pad = SUBLANES
    halo = 2 * SUBLANES

    @pl.when(pl.program_id(1) == 0)
    def _():
        xb_ref[...] = jnp.zeros_like(xb_ref)
        st_ref[...] = jnp.zeros_like(st_ref)
        for i in range(SSM_CONV):
            w8_ref[i] = jnp.broadcast_to(cw_ref[i:i + 1, :], (pad, CONV_DIM))
        w8_ref[SSM_CONV] = jnp.broadcast_to(cb_ref[...], (pad, CONV_DIM))

    xb_ref[halo:halo + t, :] = xbc_ref[...]
    sel_r = lax.broadcasted_iota(jnp.int32, ((SSM_CONV - 1) * t, xb_ref.shape[0]), 1)
    sel_t = lax.broadcasted_iota(jnp.int32, ((SSM_CONV - 1) * t, xb_ref.shape[0]), 0)
    shift = (sel_t >> (t.bit_length() - 1)) + 1
    selector = jnp.where(sel_r == halo + (sel_t & (t - 1)) - shift, 1.0, 0.0).astype(BF16)
    cw = 512
    for c in range(CONV_DIM // cw):
        sl = slice(c * cw, (c + 1) * cw)
        back = jnp.dot(selector, xb_ref[:, sl], preferred_element_type=F32)
        acc = w8_ref[SSM_CONV, :, sl][None] + w8_ref[SSM_CONV - 1, :, sl][None] * (
            xbc_ref[:, sl].astype(F32).reshape(t // pad, pad, cw))
        for i in range(1, SSM_CONV):
            acc = acc + w8_ref[SSM_CONV - 1 - i, :, sl][None] * (
                back[(i - 1) * t:i * t].reshape(t // pad, pad, cw))
        xc_ref[:, sl] = _silu(acc).reshape(t, cw)
    xb_ref[0:halo, :] = xb_ref[t:t + halo, :]

    lane = lax.broadcasted_iota(jnp.int32, (t, LANES), 1)
    row = lax.broadcasted_iota(jnp.int32, (t, LANES), 0)
    head_ok = lane < SSM_HEADS
    x_dt = dt_ref[...] + dtb_ref[...]
    dt = jnp.maximum(x_dt, 0.0) + jnp.log1p(jnp.exp(-jnp.abs(x_dt)))
    dt = jnp.where(head_ok, dt, 0.0)
    a_dt = dt * (-jnp.exp(alog_ref[...]))
    tril = row >= lane
    cs = jnp.dot(tril.astype(F32), a_dt, preferred_element_type=F32,
                 precision=lax.Precision.HIGHEST)
    cs_t = cs.T
    dt_t = dt.T
    cs_last = cs[t - 1:t, :]
    dte = jnp.exp(cs_last - cs)
    expand_rows = jnp.concatenate(
        [dt * dte, jnp.exp(cs), jnp.broadcast_to(jnp.exp(cs_last), (SUBLANES, LANES))], axis=0)
    hi = expand_rows.astype(BF16)
    rem = expand_rows - hi.astype(F32)
    mid = rem.astype(BF16)
    lo = (rem - mid.astype(F32)).astype(BF16)
    expand_rows3 = jnp.concatenate([hi, mid, lo], axis=1)

    gh = lax.broadcasted_iota(jnp.int32, (LANES, GROUP_W), 0)
    gc = lax.broadcasted_iota(jnp.int32, (LANES, GROUP_W), 1) >> (SSM_HEAD_DIM.bit_length() - 1)
    half = lax.broadcasted_iota(jnp.int32, (t, 2 * SSM_HEAD_DIM), 1) < SSM_HEAD_DIM

    for g in range(SSM_GROUPS):
        sl = slice(g * GROUP_W, (g + 1) * GROUP_W)
        expand = jnp.where(gh == gc + g * SSM_HPG, 1.0, 0.0).astype(BF16)
        ex = jnp.dot(expand_rows3, jnp.concatenate([expand, expand, expand], axis=0),
                     preferred_element_type=F32)
        dtdte_e, ecs_e, cd_e = ex[0:t], ex[t:2 * t], ex[2 * t:2 * t + 1]
        xs = xc_ref[:, sl]
        bm = xc_ref[:, D_INNER + g * SSM_STATE:D_INNER + (g + 1) * SSM_STATE]
        cm = xc_ref[:, D_INNER + (SSM_GROUPS + g) * SSM_STATE:D_INNER + (SSM_GROUPS + g + 1) * SSM_STATE]
        bm16, cm16 = bm.astype(BF16), cm.astype(BF16)
        xs16 = xs.astype(BF16)
        cb = _dot_nt(cm16, bm16)
        h_in = st_ref[:, sl]
        y = jnp.dot(cm16, h_in.astype(BF16), preferred_element_type=F32) * ecs_e
        new_states = jnp.dot(bm.T.astype(BF16), (xs * dtdte_e).astype(BF16), preferred_element_type=F32)
        st_ref[:, sl] = h_in * cd_e + new_states
        yd = []
        for j in range(SSM_HPG // 2):
            h1 = g * SSM_HPG + 2 * j
            ms = []
            for hh in (h1, h1 + 1):
                dec = jnp.exp(jnp.where(tril, cs[:, hh:hh + 1] - cs_t[hh:hh + 1, :], -jnp.inf))
                ms.append((cb * (dec * dt_t[hh:hh + 1, :])).astype(BF16))
            xp = xs16[:, 2 * j * SSM_HEAD_DIM:(2 * j + 2) * SSM_HEAD_DIM]
            zero = jnp.zeros_like(xp)
            blockdiag = jnp.concatenate([jnp.where(half, xp, zero), jnp.where(half, zero, xp)], axis=0)
            yd.append(jnp.dot(jnp.concatenate(ms, axis=1), blockdiag, preferred_element_type=F32))
        y = y + jnp.concatenate(yd, axis=1) + dskip_ref[:, sl] * xs
        y = y * _silu(z_ref[:, sl].astype(F32))
        y = y * lax.rsqrt(jnp.mean(y * y, axis=-1, keepdims=True) + EPS)
        o_ref[:, sl] = (y * ng_ref[:, sl]).astype(o_ref.dtype)


def _ssd(z, xbc, dt128, conv_w, conv_b, dt_bias, a_log, d_skip, norm_g, batch):
    m = z.shape[0]
    assert xbc.dtype == BF16
    nblk = m // batch // SSD_T
    pad_heads = lambda v: jnp.pad(v, (0, LANES - SSM_HEADS)).reshape(1, LANES)
    row_spec = lambda w: pl.BlockSpec((SSD_T, w), lambda b, i: (b * nblk + i, 0))
    par_spec = lambda r, w: pl.BlockSpec((r, w), lambda b, i: (0, 0))
    return pl.pallas_call(
        _ssd_kernel,
        grid=(batch, nblk),
        in_specs=[row_spec(D_INNER), row_spec(CONV_DIM), row_spec(LANES),
                  par_spec(SSM_CONV, CONV_DIM), par_spec(1, CONV_DIM),
                  par_spec(1, LANES), par_spec(1, LANES), par_spec(1, D_INNER), par_spec(1, D_INNER)],
        out_specs=row_spec(D_INNER),
        out_shape=jax.ShapeDtypeStruct((m, D_INNER), BF16),
        scratch_shapes=[pltpu.VMEM((2 * SSD_T, CONV_DIM), BF16),
                        pltpu.VMEM((SSD_T, CONV_DIM), F32),
                        pltpu.VMEM((SSM_STATE, D_INNER), F32),
                        pltpu.VMEM((SSM_CONV + 1, SUBLANES, CONV_DIM), F32)],
        compiler_params=_cparams("arbitrary", "arbitrary"),
        name="ssd_mixer",
    )(z, xbc, dt128, conv_w, conv_b.reshape(1, CONV_DIM), pad_heads(dt_bias), pad_heads(a_log),
      jnp.repeat(d_skip, SSM_HEAD_DIM).reshape(1, D_INNER), norm_g.reshape(1, D_INNER))


def _dsa_kernel(q_ref, qi_ref, w_ref, k_ref, v_ref, ki_ref, o_ref,
                keys_ref, bias_ref, qs_ref, qh_ref, *, top_k, seq):
    qb, kt = Q_BLOCK, KEY_TILE
    blk = pl.program_id(1)
    ntiles = (blk * qb + qb + kt - 1) // kt
    lane_q = lax.broadcasted_iota(jnp.int32, (1, qb), 1)
    limit = (((blk * qb + lane_q) >> (CHUNK.bit_length() - 1)) + 1) * CHUNK
    pos_t = lax.broadcasted_iota(jnp.int32, (kt, qb), 0)
    fold_rows = 64

    def count(pred):
        def body(t, acc):
            x = jnp.where(pred(t), 1.0, 0.0)
            return acc + jnp.sum(x.reshape(kt // fold_rows, fold_rows, qb), axis=0)
        acc = lax.fori_loop(0, ntiles, body, jnp.zeros((fold_rows, qb), F32))
        return jnp.sum(acc, axis=0, keepdims=True)

    w_t = w_ref[...].T
    lane_d = lax.broadcasted_iota(jnp.int32, (qb, LANES), 1)
    for p in range(IDX_HEADS // 2):
        qp = qi_ref[:, p * LANES:(p + 1) * LANES]
        zero = jnp.zeros_like(qp)
        qh_ref[p, 0:qb, :] = jnp.where(lane_d < IDX_HEAD_DIM, qp, zero)
        qh_ref[p, qb:2 * qb, :] = jnp.where(lane_d < IDX_HEAD_DIM, zero, qp)

    def score_tile(t, c):
        kx = ki_ref[pl.ds(pl.multiple_of(t * kt, kt), kt), :]
        acc = jnp.zeros((kt, qb), F32)
        for p in range(IDX_HEADS // 2):
            s2 = _dot_nt(kx, qh_ref[p])
            for e in range(2):
                hd = 2 * p + e
                acc = acc + jnp.maximum(s2[:, e * qb:(e + 1) * qb], 0.0) * w_t[hd:hd + 1, :]
        score = acc * IDX_SCALE + 0.0
        bits = pltpu.bitcast(score, jnp.int32)
        key = jnp.where(bits < 0, bits ^ jnp.int32(0x7FFFFFFF), bits)
        keys_ref[t] = jnp.where(t * kt + pos_t < limit, key, jnp.int32(INT_MIN))
        return c

    lax.fori_loop(0, ntiles, score_tile, 0)

    def bit_step(it, tb):
        cand = tb | lax.shift_left(jnp.int32(1), jnp.int32(31) - it)
        cand_s = cand ^ jnp.int32(INT_MIN)
        cnt = count(lambda t: keys_ref[t] >= cand_s)
        return jnp.where(cnt >= float(top_k), cand, tb)

    tb = lax.fori_loop(0, 32, bit_step, jnp.zeros((1, qb), jnp.int32))
    thr = tb ^ jnp.int32(INT_MIN)

    cnt_gt = count(lambda t: keys_ref[t] > thr)
    cnt_ge = cnt_gt + count(lambda t: keys_ref[t] == thr)
    excess = (cnt_ge > float(top_k)) & (thr != jnp.int32(INT_MIN))
    need = float(top_k) - cnt_gt

    def tie_cut():
        def step(it, mp):
            cand = mp | lax.shift_left(jnp.int32(1), jnp.int32(seq.bit_length() - 1) - it)
            cnt = count(lambda t: (keys_ref[t] == thr) & (t * kt + pos_t < cand))
            return jnp.where(cnt < need, cand, mp)
        return lax.fori_loop(0, seq.bit_length(), step, jnp.zeros((1, qb), jnp.int32))

    any_excess = jnp.max(jnp.where(excess, 1.0, 0.0)) > 0.0
    last_tie = lax.cond(any_excess, tie_cut, lambda: jnp.full((1, qb), seq, jnp.int32))

    def bias_tile(t, c):
        key = keys_ref[t]
        pos = t * kt + pos_t
        sel = (key > thr) | ((key == thr) & (pos <= last_tie))
        b_t = jnp.where(sel & (pos < limit), 0.0, NEG_BIG)
        for c4 in range(kt // LANES):
            bias_ref[t, :, c4 * LANES:(c4 + 1) * LANES] = b_t[c4 * LANES:(c4 + 1) * LANES, :].T
        return c

    lax.fori_loop(0, ntiles, bias_tile, 0)

    log2_scale = ATTN_SCALE * float(np.log2(np.e))
    rows = ATTN_QPK * qb
    n_chain = 2
    for g0 in range(0, ATTN_KV_HEADS, n_chain):
        for c in range(n_chain):
            for j in range(ATTN_QPK):
                hq = (g0 + c) * ATTN_QPK + j
                qs_ref[c, j * qb:(j + 1) * qb, :] = q_ref[:, hq * ATTN_HEAD_DIM:(hq + 1) * ATTN_HEAD_DIM]

        def att_step(t, carry, span):
            width = span * kt
            ks = pl.ds(pl.multiple_of(t * kt, kt), width)
            bias = jnp.concatenate([bias_ref[t + u] for u in range(span)], axis=1)[None]
            new = []
            for c in range(n_chain):
                m_i, l_i, acc = carry[c]
                cols = slice((g0 + c) * ATTN_HEAD_DIM, (g0 + c + 1) * ATTN_HEAD_DIM)
                s = _dot_nt(qs_ref[c], k_ref[ks, cols])
                s = ((s * log2_scale).reshape(ATTN_QPK, qb, width) + bias).reshape(rows, width)
                m_new = jnp.maximum(m_i, jnp.max(s, axis=1, keepdims=True))
                p = jnp.exp2(s - m_new)
                alpha = jnp.exp2(m_i - m_new)
                new.append((m_new, alpha * l_i + jnp.sum(p, axis=1, keepdims=True),
                            alpha * acc + jnp.dot(p.astype(BF16), v_ref[ks, cols], preferred_element_type=F32)))
            return tuple(new)

        init = tuple((jnp.full((rows, 1), NEG_BIG, F32), jnp.zeros((rows, 1), F32),
                      jnp.zeros((rows, ATTN_HEAD_DIM), F32)) for _ in range(n_chain))
        res = lax.fori_loop(0, ntiles // 2, lambda tp, carry: att_step(2 * tp, carry, 2), init)
        res = lax.cond(ntiles % 2 == 1, lambda carry: att_step(ntiles - 1, carry, 1), lambda carry: carry, res)
        for c in range(n_chain):
            out = res[c][2] / res[c][1]
            for j in range(ATTN_QPK):
                hq = (g0 + c) * ATTN_QPK + j
                o_ref[:, hq * ATTN_HEAD_DIM:(hq + 1) * ATTN_HEAD_DIM] = out[j * qb:(j + 1) * qb].astype(o_ref.dtype)


def _dsa(qkvi, kidx2, widx, batch):
    m = qkvi.shape[0]
    seq = m // batch
    nb = seq // Q_BLOCK
    top_k = min(TOPK_MAX, seq // 4)
    c_k, c_v, c_qi = ATTN_DIM // KV_DIM, ATTN_DIM // KV_DIM + 1, (ATTN_DIM + 2 * KV_DIM) // IDX_DIM
    return pl.pallas_call(
        functools.partial(_dsa_kernel, top_k=top_k, seq=seq),
        grid=(batch, nb),
        in_specs=[pl.BlockSpec((Q_BLOCK, ATTN_DIM), lambda b, i: (b * nb + i, 0)),
                  pl.BlockSpec((Q_BLOCK, IDX_DIM), lambda b, i: (b * nb + i, c_qi)),
                  pl.BlockSpec((Q_BLOCK, LANES), lambda b, i: (b * nb + i, 0)),
                  pl.BlockSpec((seq, KV_DIM), lambda b, i: (b, c_k)),
                  pl.BlockSpec((seq, KV_DIM), lambda b, i: (b, c_v)),
                  pl.BlockSpec((seq, LANES), lambda b, i: (b, 0))],
        out_specs=pl.BlockSpec((Q_BLOCK, ATTN_DIM), lambda b, i: (b * nb + i, 0)),
        out_shape=jax.ShapeDtypeStruct((m, ATTN_DIM), BF16),
        scratch_shapes=[pltpu.VMEM((seq // KEY_TILE, KEY_TILE, Q_BLOCK), jnp.int32),
                        pltpu.VMEM((seq // KEY_TILE, Q_BLOCK, KEY_TILE), F32),
                        pltpu.VMEM((2, ATTN_QPK * Q_BLOCK, ATTN_HEAD_DIM), BF16),
                        pltpu.VMEM((IDX_HEADS // 2, 2 * Q_BLOCK, LANES), BF16)],
        compiler_params=_cparams("arbitrary", "arbitrary"),
        name="dsa_mixer",
    )(qkvi, qkvi, widx, qkvi, qkvi, kidx2)


def _merge_kernel(ys_ref, ya_ref, gs_ref, ga_ref, ws_ref, wa_ref, o_ref, wss_ref, was_ref):
    @pl.when(pl.program_id(1) == 0)
    def _():
        _cast_weight(ws_ref, wss_ref)
        _cast_weight(wa_ref, was_ref)

    ps = jnp.dot(ys_ref[...], wss_ref[...], preferred_element_type=F32)
    pa = jnp.dot(ya_ref[...], was_ref[...], preferred_element_type=F32)
    o_ref[...] = (gs_ref[...].astype(F32) * ps + ga_ref[...].astype(F32) * pa).astype(o_ref.dtype)


def _merge(y_ssm, y_attn, gates, w_ssm_out, w_attn_out, layer, tn=512, tm=512):
    m = y_ssm.shape[0]
    tm = min(tm, m)
    nt = D_MODEL // tn
    return pl.pallas_call(
        _merge_kernel,
        grid=(nt, m // tm),
        in_specs=[pl.BlockSpec((tm, D_INNER), lambda j, i: (i, 0)),
                  pl.BlockSpec((tm, ATTN_DIM), lambda j, i: (i, 0)),
                  pl.BlockSpec((tm, tn), lambda j, i: (i, j)),
                  pl.BlockSpec((tm, tn), lambda j, i: (i, nt + j)),
                  pl.BlockSpec((None, D_INNER, tn), lambda j, i: (layer, 0, j)),
                  pl.BlockSpec((None, ATTN_DIM, tn), lambda j, i: (layer, 0, j))],
        out_specs=pl.BlockSpec((tm, tn), lambda j, i: (i, j)),
        out_shape=jax.ShapeDtypeStruct((m, D_MODEL), BF16),
        scratch_shapes=[pltpu.VMEM((D_INNER, tn), BF16), pltpu.VMEM((ATTN_DIM, tn), BF16)],
        compiler_params=_cparams("arbitrary", "arbitrary"),
        name="branch_merge",
    )(y_ssm, y_attn, gates, gates, w_ssm_out, w_attn_out)


def _cast_kernel(w_ref, o_ref):
    o_ref[...] = w_ref[...].astype(o_ref.dtype)


def _cast_layer(w_all, layer, tr=512):
    _, k, n = w_all.shape
    return pl.pallas_call(
        _cast_kernel,
        grid=(k // tr,),
        in_specs=[pl.BlockSpec((None, tr, n), lambda i: (layer, i, 0))],
        out_specs=pl.BlockSpec((tr, n), lambda i: (i, 0)),
        out_shape=jax.ShapeDtypeStruct((k, n), BF16),
        compiler_params=_cparams("arbitrary"),
        name="weight_cast",
    )(w_all)


def _rowmm_kernel(a_ref, w_ref, x_ref, gpost_ref, gnext_ref, xo_ref, *rest, emit_next):
    ho_ref = rest[0] if emit_next else None
    y = jnp.dot(a_ref[...], w_ref[...], preferred_element_type=F32)
    x_new = x_ref[...] + _rms(y, gpost_ref[...])
    xo_ref[...] = x_new
    if emit_next:
        ho_ref[...] = _rms(x_new, gnext_ref[...]).astype(ho_ref.dtype)


def _rowmm(a, w_all, layer, x, g_post, g_next, tm=512):
    m, k = a.shape
    w16 = _cast_layer(w_all, layer)
    d = x.shape[1]
    tm = min(tm, m)
    emit_next = g_next is not None
    if g_next is None:
        g_next = g_post
    row = pl.BlockSpec((tm, d), lambda i: (i, 0))
    par = pl.BlockSpec((1, d), lambda i: (0, 0))
    out_specs = [row, row] if emit_next else [row]
    out_shape = [jax.ShapeDtypeStruct((m, d), F32)] + ([jax.ShapeDtypeStruct((m, d), BF16)] if emit_next else [])
    outs = pl.pallas_call(
        functools.partial(_rowmm_kernel, emit_next=emit_next),
        grid=(m // tm,),
        in_specs=[pl.BlockSpec((tm, k), lambda i: (i, 0)),
                  pl.BlockSpec((k, d), lambda i: (0, 0), pipeline_mode=pl.Buffered(1)),
                  row, par, par],
        out_specs=out_specs,
        out_shape=out_shape,
        compiler_params=_cparams("arbitrary"),
        name="out_proj_norm_residual",
    )(a, w16, x, g_post.reshape(1, d), g_next.reshape(1, d))
    return (outs[0], outs[1]) if emit_next else (outs[0], None)


def _ffn_up_kernel(a_ref, wg_ref, wv_ref, cwg_ref, cwv_ref, cbg_ref, cbv_ref, o_ref,
                   wgs_ref, wvs_ref, ug_ref, uv_ref, *, tiles_per_seq):
    pad = SUBLANES
    tm = a_ref.shape[0]
    i = pl.program_id(1)

    @pl.when(i == 0)
    def _():
        _cast_weight(wg_ref, wgs_ref)
        _cast_weight(wv_ref, wvs_ref)

    @pl.when(i % tiles_per_seq == 0)
    def _():
        ug_ref[0:pad, :] = jnp.zeros((pad, ug_ref.shape[1]), F32)
        uv_ref[0:pad, :] = jnp.zeros((pad, uv_ref.shape[1]), F32)

    a = a_ref[...]

    def conv(w_s, u_ref, cw_ref, cb_ref):
        u_ref[pad:pad + tm, :] = jnp.dot(a, w_s[...], preferred_element_type=F32)
        acc = cb_ref[...] + cw_ref[FFN_CONV - 1:FFN_CONV, :] * u_ref[pad:pad + tm, :]
        for s in range(1, FFN_CONV):
            acc = acc + cw_ref[FFN_CONV - 1 - s:FFN_CONV - s, :] * u_ref[pad - s:pad - s + tm, :]
        u_ref[0:pad, :] = u_ref[tm:tm + pad, :]
        return acc

    gte = conv(wgs_ref, ug_ref, cwg_ref, cbg_ref)
    val = conv(wvs_ref, uv_ref, cwv_ref, cbv_ref)
    o_ref[...] = (jax.nn.gelu(gte, approximate=True) * val).astype(o_ref.dtype)


def _ffn_up(h, w_up, conv_w, conv_b, layer, batch, tn=512, tm=1024):
    m, k = h.shape
    seq = m // batch
    tm = min(tm, seq)
    nt = D_FF // tn
    wspec = lambda shift: pl.BlockSpec((None, k, tn), lambda j, i: (layer, 0, j + shift))
    cwspec = lambda shift: pl.BlockSpec((FFN_CONV, tn), lambda j, i: (0, j + shift))
    cbspec = lambda shift: pl.BlockSpec((1, tn), lambda j, i: (0, j + shift))
    return pl.pallas_call(
        functools.partial(_ffn_up_kernel, tiles_per_seq=seq // tm),
        grid=(nt, m // tm),
        in_specs=[pl.BlockSpec((tm, k), lambda j, i: (i, 0)),
                  wspec(0), wspec(nt), cwspec(0), cwspec(nt), cbspec(0), cbspec(nt)],
        out_specs=pl.BlockSpec((tm, tn), lambda j, i: (i, j)),
        out_shape=jax.ShapeDtypeStruct((m, D_FF), BF16),
        scratch_shapes=[pltpu.VMEM((k, tn), BF16), pltpu.VMEM((k, tn), BF16),
                        pltpu.VMEM((tm + 2 * SUBLANES, tn), F32), pltpu.VMEM((tm + 2 * SUBLANES, tn), F32)],
        compiler_params=_cparams("arbitrary", "arbitrary"),
        name="ffn_up_conv_geglu",
    )(h, w_up, w_up, conv_w, conv_w, conv_b.reshape(1, 2 * D_FF), conv_b.reshape(1, 2 * D_FF))


def kernel(x, norm_mix_pre, norm_mix_post, norm_ffn_pre, norm_ffn_post, w_in, b_gate, conv_xbc_w, conv_xbc_b, dt_bias, a_log, d_skip, ssm_norm, w_ssm_out, w_attn_out, w_mix_out, w_up, conv_ffn_w, conv_ffn_b, w_down):
    batch, seq, d = x.shape
    depth = w_in.shape[0]
    xf = x.reshape(batch * seq, d)
    h = _norm(xf, norm_mix_pre[0])
    w_t = jnp.swapaxes(w_in, 1, 2)
    for l in range(depth):
        z = _proj(h, w_t, l, OFF_Z, D_INNER, BF16)
        xbc = _proj(h, w_t, l, OFF_XBC, CONV_DIM, BF16)
        qkvi = _proj(h, w_t, l, OFF_Q, ATTN_DIM + 2 * KV_DIM + IDX_DIM, BF16)
        gates = _proj(h, w_t, l, OFF_GATE, N_BRANCHES * D_MODEL, BF16, bias=b_gate[l])
        dt128, kidx2, widx = _smalls(h, w_t, l)
        y_ssm = _ssd(z, xbc, dt128, conv_xbc_w[l], conv_xbc_b[l], dt_bias[l], a_log[l], d_skip[l],
                     ssm_norm[l], batch)
        y_attn = _dsa(qkvi, kidx2, widx, batch)
        merged = _merge(y_ssm, y_attn, gates, w_ssm_out, w_attn_out, l)
        xf, h2 = _rowmm(merged, w_mix_out, l, xf, norm_mix_post[l], norm_ffn_pre[l])
        act = _ffn_up(h2, w_up, conv_ffn_w[l], conv_ffn_b[l], l, batch)
        xf, h = _rowmm(act, w_down, l, xf, norm_ffn_post[l], norm_mix_pre[l + 1] if l + 1 < depth else None)
    return xf.reshape(batch, seq, d)
```

```python
import functools

import numpy as np
import jax
import jax.numpy as jnp
from jax import lax
from jax.experimental import pallas as pl
from jax.experimental.pallas import tpu as pltpu

F32 = jnp.float32
BF16 = jnp.bfloat16

D_MODEL = 2048
EPS = 1e-6

D_INNER = 2 * D_MODEL
SSM_HEAD_DIM = 64
SSM_HEADS = D_INNER // SSM_HEAD_DIM
SSM_GROUPS = 8
SSM_HPG = SSM_HEADS // SSM_GROUPS
SSM_STATE = 128
SSM_CONV = 4
CONV_DIM = D_INNER + 2 * SSM_GROUPS * SSM_STATE
GROUP_W = SSM_HPG * SSM_HEAD_DIM
SSD_T = 128

CHUNK = 64
ATTN_HEAD_DIM = 128
ATTN_Q_HEADS = 16
ATTN_KV_HEADS = 4
ATTN_QPK = ATTN_Q_HEADS // ATTN_KV_HEADS
ATTN_DIM = ATTN_Q_HEADS * ATTN_HEAD_DIM
KV_DIM = ATTN_KV_HEADS * ATTN_HEAD_DIM
ATTN_SCALE = ATTN_HEAD_DIM ** -0.5
IDX_HEADS = 16
IDX_HEAD_DIM = 64
IDX_DIM = IDX_HEADS * IDX_HEAD_DIM
IDX_SCALE = (IDX_HEAD_DIM ** -0.5) * (IDX_HEADS ** -0.5)
TOPK_MAX = 256
Q_BLOCK = 128
KEY_TILE = 512

N_BRANCHES = 2
D_FF = 2 * D_MODEL
FFN_CONV = 3

IN_SPLITS = (D_INNER, CONV_DIM, SSM_HEADS, ATTN_DIM, KV_DIM, KV_DIM,
             IDX_DIM, IDX_HEAD_DIM, IDX_HEADS, N_BRANCHES * D_MODEL)
_OFFS = [0] + [int(o) for o in np.cumsum(IN_SPLITS)]
OFF_Z, OFF_XBC, OFF_DT, OFF_Q, OFF_K, OFF_V, OFF_QIDX, OFF_KIDX, OFF_WIDX, OFF_GATE, D_IN_PROJ = _OFFS

LANES = 128
SUBLANES = 8
VMEM_LIMIT = 56 * 2 ** 20

INT_MIN = -2 ** 31
NEG_BIG = -1e30


def _cparams(*sem):
    return pltpu.CompilerParams(dimension_semantics=sem, vmem_limit_bytes=VMEM_LIMIT)


def _rms(x, g):
    return x * lax.rsqrt(jnp.mean(x * x, axis=-1, keepdims=True) + EPS) * g


def _norm_kernel(x_ref, g_ref, o_ref):
    o_ref[...] = _rms(x_ref[...], g_ref[...]).astype(o_ref.dtype)


def _norm(x, g, tm=512):
    m, d = x.shape
    return pl.pallas_call(
        _norm_kernel,
        grid=(m // tm,),
        in_specs=[pl.BlockSpec((tm, d), lambda i: (i, 0)),
                  pl.BlockSpec((1, d), lambda i: (0, 0))],
        out_specs=pl.BlockSpec((tm, d), lambda i: (i, 0)),
        out_shape=jax.ShapeDtypeStruct((m, d), BF16),
        compiler_params=_cparams("arbitrary"),
        name="rmsnorm",
    )(x, g.reshape(1, d))


def _cast_weight(w_ref, wsc_ref, kc=256):
    def body(i, c):
        rows = pl.ds(pl.multiple_of(i * kc, kc), kc)
        wsc_ref[rows, :] = w_ref[rows, :].astype(BF16)
        return c

    lax.fori_loop(0, wsc_ref.shape[0] // kc, body, 0)


def _dot_nt(a, b):
    return lax.dot_general(a, b, (((1,), (1,)), ((), ())), preferred_element_type=F32)


def _proj_kernel(*refs, gate, tm):
    refs = list(refs)
    h_ref = refs.pop(0)
    w_ref = refs.pop(0)
    b_ref = refs.pop(0) if gate else None
    o_ref, wsc_ref = refs
    i = pl.program_id(1)

    @pl.when(i == 0)
    def _():
        wsc_ref[...] = w_ref[0].astype(BF16)

    a = h_ref[pl.ds(pl.multiple_of(i * tm, tm), tm), :]
    acc = _dot_nt(a, wsc_ref[...])
    if gate:
        acc = jax.nn.sigmoid(acc + b_ref[...])
    o_ref[...] = acc.astype(o_ref.dtype)


def _proj(h, w_t, layer, row_start, n_rows, out_dtype, bias=None, tn=512, tm=2048):
    m, k = h.shape
    tm = min(tm, m)
    nt = n_rows // tn
    assert row_start % SUBLANES == 0 and n_rows % tn == 0
    in_specs = [pl.BlockSpec((m, k), lambda j, i: (0, 0), pipeline_mode=pl.Buffered(1)),
                pl.BlockSpec((pl.Element(1), pl.Element(tn), pl.Element(k)),
                             lambda j, i: (layer, pl.multiple_of(row_start + j * tn, SUBLANES), 0))]
    args = [h, w_t]
    if bias is not None:
        in_specs.append(pl.BlockSpec((1, tn), lambda j, i: (0, j)))
        args.append(bias.reshape(1, n_rows))
    return pl.pallas_call(
        functools.partial(_proj_kernel, gate=bias is not None, tm=tm),
        grid=(nt, m // tm),
        in_specs=in_specs,
        out_specs=pl.BlockSpec((tm, tn), lambda j, i: (i, j)),
        out_shape=jax.ShapeDtypeStruct((m, n_rows), out_dtype),
        scratch_shapes=[pltpu.VMEM((tn, k), BF16)],
        compiler_params=_cparams("arbitrary", "arbitrary"),
        name="in_proj",
    )(*args)


def _silu(x):
    h = 0.5 * x
    return h + h * jnp.tanh(h)


def _smalls_kernel(a_ref, wdt_ref, wk_ref, ww_ref, dt_ref, kidx_ref, widx_ref):
    a = a_ref[...]
    k = a.shape[1]

    def pad_rows(w):
        return jnp.concatenate([w, jnp.zeros((LANES - w.shape[0], k), BF16)], axis=0)

    wk = wk_ref[...].astype(BF16)
    dt_ref[...] = _dot_nt(a, pad_rows(wdt_ref[...].astype(BF16)))
    widx_ref[...] = _dot_nt(a, pad_rows(ww_ref[...].astype(BF16)))
    kidx_ref[...] = _dot_nt(a, jnp.concatenate([wk, wk], axis=0)).astype(kidx_ref.dtype)


def _smalls(h, w_t, layer, tm=1024):
    m, k = h.shape
    tm = min(tm, m)
    assert 2 * IDX_HEAD_DIM == LANES and IDX_HEADS % 16 == 0
    wspec = lambda start, n: pl.BlockSpec((None, n, k), lambda i: (layer, start // n, 0))
    ospec = pl.BlockSpec((tm, LANES), lambda i: (i, 0))
    return pl.pallas_call(
        _smalls_kernel,
        grid=(m // tm,),
        in_specs=[pl.BlockSpec((tm, k), lambda i: (i, 0)),
                  wspec(OFF_DT, SSM_HEADS), wspec(OFF_KIDX, IDX_HEAD_DIM), wspec(OFF_WIDX, IDX_HEADS)],
        out_specs=[ospec, ospec, ospec],
        out_shape=[jax.ShapeDtypeStruct((m, LANES), F32),
                   jax.ShapeDtypeStruct((m, LANES), BF16),
                   jax.ShapeDtypeStruct((m, LANES), F32)],
        compiler_params=_cparams("arbitrary"),
        name="in_proj_narrow",
    )(h, w_t, w_t, w_t)


def _ssd_kernel(z_ref, xbc_ref, dt_ref, cw_ref, cb_ref, dtb_ref, alog_ref, dskip_ref, ng_ref,
                o_ref, xb_ref, xc_ref, st_ref, w8_ref):
    t = SSD_T
    pad = SUBLANES
    halo = 2 * SUBLANES

    @pl.when(pl.program_id(1) == 0)
    def _():
        xb_ref[...] = jnp.zeros_like(xb_ref)
        st_ref[...] = jnp.zeros_like(st_ref)
        for i in range(SSM_CONV):
            w8_ref[i] = jnp.broadcast_to(cw_ref[i:i + 1, :], (pad, CONV_DIM))
        w8_ref[SSM_CONV] = jnp.broadcast_to(cb_ref[...], (pad, CONV_DIM))

    xb_ref[halo:halo + t, :] = xbc_ref[...]
    sel_r = lax.broadcasted_iota(jnp.int32, ((SSM_CONV - 1) * t, xb_ref.shape[0]), 1)
    sel_t = lax.broadcasted_iota(jnp.int32, ((SSM_CONV - 1) * t, xb_ref.shape[0]), 0)
    shift = (sel_t >> (t.bit_length() - 1)) + 1
    selector = jnp.where(sel_r == halo + (sel_t & (t - 1)) - shift, 1.0, 0.0).astype(BF16)
    cw = 512
    for c in range(CONV_DIM // cw):
        sl = slice(c * cw, (c + 1) * cw)
        back = jnp.dot(selector, xb_ref[:, sl], preferred_element_type=F32)
        acc = w8_ref[SSM_CONV, :, sl][None] + w8_ref[SSM_CONV - 1, :, sl][None] * (
            xbc_ref[:, sl].astype(F32).reshape(t // pad, pad, cw))
        for i in range(1, SSM_CONV):
            acc = acc + w8_ref[SSM_CONV - 1 - i, :, sl][None] * (
                back[(i - 1) * t:i * t].reshape(t // pad, pad, cw))
        xc_ref[:, sl] = _silu(acc).reshape(t, cw)
    xb_ref[0:halo, :] = xb_ref[t:t + halo, :]

    lane = lax.broadcasted_iota(jnp.int32, (t, LANES), 1)
    row = lax.broadcasted_iota(jnp.int32, (t, LANES), 0)
    head_ok = lane < SSM_HEADS
    x_dt = dt_ref[...] + dtb_ref[...]
    dt = jnp.maximum(x_dt, 0.0) + jnp.log1p(jnp.exp(-jnp.abs(x_dt)))
    dt = jnp.where(head_ok, dt, 0.0)
    a_dt = dt * (-jnp.exp(alog_ref[...]))
    tril = row >= lane
    cs = jnp.dot(tril.astype(F32), a_dt, preferred_element_type=F32,
                 precision=lax.Precision.HIGHEST)
    cs_t = cs.T
    dt_t = dt.T
    cs_last = cs[t - 1:t, :]
    dte = jnp.exp(cs_last - cs)
    expand_rows = jnp.concatenate(
        [dt * dte, jnp.exp(cs), jnp.broadcast_to(jnp.exp(cs_last), (SUBLANES, LANES))], axis=0)
    hi = expand_rows.astype(BF16)
    rem = expand_rows - hi.astype(F32)
    mid = rem.astype(BF16)
    lo = (rem - mid.astype(F32)).astype(BF16)
    expand_rows3 = jnp.concatenate([hi, mid, lo], axis=1)

    gh = lax.broadcasted_iota(jnp.int32, (LANES, GROUP_W), 0)
    gc = lax.broadcasted_iota(jnp.int32, (LANES, GROUP_W), 1) >> (SSM_HEAD_DIM.bit_length() - 1)
    half = lax.broadcasted_iota(jnp.int32, (t, 2 * SSM_HEAD_DIM), 1) < SSM_HEAD_DIM

    for g in range(SSM_GROUPS):
        sl = slice(g * GROUP_W, (g + 1) * GROUP_W)
        expand = jnp.where(gh == gc + g * SSM_HPG, 1.0, 0.0).astype(BF16)
        ex = jnp.dot(expand_rows3, jnp.concatenate([expand, expand, expand], axis=0),
                     preferred_element_type=F32)
        dtdte_e, ecs_e, cd_e = ex[0:t], ex[t:2 * t], ex[2 * t:2 * t + 1]
        xs = xc_ref[:, sl]
        bm = xc_ref[:, D_INNER + g * SSM_STATE:D_INNER + (g + 1) * SSM_STATE]
        cm = xc_ref[:, D_INNER + (SSM_GROUPS + g) * SSM_STATE:D_INNER + (SSM_GROUPS + g + 1) * SSM_STATE]
        bm16, cm16 = bm.astype(BF16), cm.astype(BF16)
        xs16 = xs.astype(BF16)
        cb = _dot_nt(cm16, bm16)
        h_in = st_ref[:, sl]
        y = jnp.dot(cm16, h_in.astype(BF16), preferred_element_type=F32) * ecs_e
        new_states = jnp.dot(bm.T.astype(BF16), (xs * dtdte_e).astype(BF16), preferred_element_type=F32)
        st_ref[:, sl] = h_in * cd_e + new_states
        yd = []
        for j in range(SSM_HPG // 2):
            h1 = g * SSM_HPG + 2 * j
            ms = []
            for hh in (h1, h1 + 1):
                dec = jnp.exp(jnp.where(tril, cs[:, hh:hh + 1] - cs_t[hh:hh + 1, :], -jnp.inf))
                ms.append((cb * (dec * dt_t[hh:hh + 1, :])).astype(BF16))
            xp = xs16[:, 2 * j * SSM_HEAD_DIM:(2 * j + 2) * SSM_HEAD_DIM]
            zero = jnp.zeros_like(xp)
            blockdiag = jnp.concatenate([jnp.where(half, xp, zero), jnp.where(half, zero, xp)], axis=0)
            yd.append(jnp.dot(jnp.concatenate(ms, axis=1), blockdiag, preferred_element_type=F32))
        y = y + jnp.concatenate(yd, axis=1) + dskip_ref[:, sl] * xs
        y = y * _silu(z_ref[:, sl].astype(F32))
        y = y * lax.rsqrt(jnp.mean(y * y, axis=-1, keepdims=True) + EPS)
        o_ref[:, sl] = (y * ng_ref[:, sl]).astype(o_ref.dtype)


def _ssd(z, xbc, dt128, conv_w, conv_b, dt_bias, a_log, d_skip, norm_g, batch):
    m = z.shape[0]
    assert xbc.dtype == BF16
    nblk = m // batch // SSD_T
    pad_heads = lambda v: jnp.pad(v, (0, LANES - SSM_HEADS)).reshape(1, LANES)
    row_spec = lambda w: pl.BlockSpec((SSD_T, w), lambda b, i: (b * nblk + i, 0))
    par_spec = lambda r, w: pl.BlockSpec((r, w), lambda b, i: (0, 0))
    return pl.pallas_call(
        _ssd_kernel,
        grid=(batch, nblk),
        in_specs=[row_spec(D_INNER), row_spec(CONV_DIM), row_spec(LANES),
                  par_spec(SSM_CONV, CONV_DIM), par_spec(1, CONV_DIM),
                  par_spec(1, LANES), par_spec(1, LANES), par_spec(1, D_INNER), par_spec(1, D_INNER)],
        out_specs=row_spec(D_INNER),
        out_shape=jax.ShapeDtypeStruct((m, D_INNER), BF16),
        scratch_shapes=[pltpu.VMEM((2 * SSD_T, CONV_DIM), BF16),
                        pltpu.VMEM((SSD_T, CONV_DIM), F32),
                        pltpu.VMEM((SSM_STATE, D_INNER), F32),
                        pltpu.VMEM((SSM_CONV + 1, SUBLANES, CONV_DIM), F32)],
        compiler_params=_cparams("arbitrary", "arbitrary"),
        name="ssd_mixer",
    )(z, xbc, dt128, conv_w, conv_b.reshape(1, CONV_DIM), pad_heads(dt_bias), pad_heads(a_log),
      jnp.repeat(d_skip, SSM_HEAD_DIM).reshape(1, D_INNER), norm_g.reshape(1, D_INNER))


def _dsa_kernel(q_ref, qi_ref, w_ref, k_ref, v_ref, ki_ref, o_ref,
                keys_ref, bias_ref, qs_ref, qh_ref, *, top_k, seq):
    qb, kt = Q_BLOCK, KEY_TILE
    blk = pl.program_id(1)
    ntiles = (blk * qb + qb + kt - 1) // kt
    lane_q = lax.broadcasted_iota(jnp.int32, (1, qb), 1)
    limit = (((blk * qb + lane_q) >> (CHUNK.bit_length() - 1)) + 1) * CHUNK
    pos_t = lax.broadcasted_iota(jnp.int32, (kt, qb), 0)
    fold_rows = 64

    def tile_pairs(step, init):
        carry = lax.fori_loop(0, ntiles // 2, lambda tp, c: step(2 * tp, 2, c), init)
        return lax.cond(ntiles % 2 == 1, lambda c: step(ntiles - 1, 1, c), lambda c: c, carry)

    def count(pred):
        def body(t, acc):
            x = jnp.where(pred(keys_ref[t], t * kt + pos_t), 1.0, 0.0)
            return acc + jnp.sum(x.reshape(kt // fold_rows, fold_rows, qb), axis=0)
        acc = lax.fori_loop(0, ntiles, body, jnp.zeros((fold_rows, qb), F32))
        return jnp.sum(acc, axis=0, keepdims=True)

    w_t = w_ref[...].T
    lane_d = lax.broadcasted_iota(jnp.int32, (qb, LANES), 1)
    for p in range(IDX_HEADS // 2):
        qp = qi_ref[:, p * LANES:(p + 1) * LANES]
        zero = jnp.zeros_like(qp)
        qh_ref[p, 0:qb, :] = jnp.where(lane_d < IDX_HEAD_DIM, qp, zero)
        qh_ref[p, qb:2 * qb, :] = jnp.where(lane_d < IDX_HEAD_DIM, zero, qp)

    def score_step(t, span, c):
        n = span * kt
        kx = ki_ref[pl.ds(pl.multiple_of(t * kt, kt), n), :]
        acc = jnp.zeros((n, qb), F32)
        for p in range(IDX_HEADS // 2):
            s2 = _dot_nt(kx, qh_ref[p])
            for e in range(2):
                hd = 2 * p + e
                acc = acc + jnp.maximum(s2[:, e * qb:(e + 1) * qb], 0.0) * w_t[hd:hd + 1, :]
        score = acc * IDX_SCALE + 0.0
        bits = pltpu.bitcast(score, jnp.int32)
        key = jnp.where(bits < 0, bits ^ jnp.int32(0x7FFFFFFF), bits)
        pos = t * kt + lax.broadcasted_iota(jnp.int32, (n, qb), 0)
        keys_ref[pl.ds(t, span)] = jnp.where(pos < limit, key, jnp.int32(INT_MIN)).reshape(span, kt, qb)
        return c

    tile_pairs(score_step, 0)

    def bit_step(it, tb):
        cand = tb | lax.shift_left(jnp.int32(1), jnp.int32(31) - it)
        cand_s = cand ^ jnp.int32(INT_MIN)
        cnt = count(lambda key, pos: key >= cand_s)
        return jnp.where(cnt >= float(top_k), cand, tb)

    tb = lax.fori_loop(0, 32, bit_step, jnp.zeros((1, qb), jnp.int32))
    thr = tb ^ jnp.int32(INT_MIN)

    cnt_gt = count(lambda key, pos: key > thr)
    cnt_ge = cnt_gt + count(lambda key, pos: key == thr)
    excess = (cnt_ge > float(top_k)) & (thr != jnp.int32(INT_MIN))
    need = float(top_k) - cnt_gt

    def tie_cut():
        def step(it, mp):
            cand = mp | lax.shift_left(jnp.int32(1), jnp.int32(seq.bit_length() - 1) - it)
            cnt = count(lambda key, pos: (key == thr) & (pos < cand))
            return jnp.where(cnt < need, cand, mp)
        return lax.fori_loop(0, seq.bit_length(), step, jnp.zeros((1, qb), jnp.int32))

    any_excess = jnp.max(jnp.where(excess, 1.0, 0.0)) > 0.0
    last_tie = lax.cond(any_excess, tie_cut, lambda: jnp.full((1, qb), seq, jnp.int32))

    def bias_tile(t, c):
        key = keys_ref[t]
        pos = t * kt + pos_t
        sel = (key > thr) | ((key == thr) & (pos <= last_tie))
        b_t = jnp.where(sel & (pos < limit), 0.0, NEG_BIG)
        for c4 in range(kt // LANES):
            bias_ref[t, :, c4 * LANES:(c4 + 1) * LANES] = b_t[c4 * LANES:(c4 + 1) * LANES, :].T
        return c

    lax.fori_loop(0, ntiles, bias_tile, 0)

    log2_scale = ATTN_SCALE * float(np.log2(np.e))
    rows = ATTN_QPK * qb
    n_chain = 4
    for g0 in range(0, ATTN_KV_HEADS, n_chain):
        for c in range(n_chain):
            for j in range(ATTN_QPK):
                hq = (g0 + c) * ATTN_QPK + j
                qs_ref[c, j * qb:(j + 1) * qb, :] = q_ref[:, hq * ATTN_HEAD_DIM:(hq + 1) * ATTN_HEAD_DIM]

        def att_step(t, span, carry):
            width = span * kt
            ks = pl.ds(pl.multiple_of(t * kt, kt), width)
            bias = jnp.concatenate([bias_ref[t + u] for u in range(span)], axis=1)[None]
            new = []
            for c in range(n_chain):
                m_i, l_i, acc = carry[c]
                cols = slice((g0 + c) * ATTN_HEAD_DIM, (g0 + c + 1) * ATTN_HEAD_DIM)
                s = _dot_nt(qs_ref[c], k_ref[ks, cols])
                s = ((s * log2_scale).reshape(ATTN_QPK, qb, width) + bias).reshape(rows, width)
                m_new = jnp.maximum(m_i, jnp.max(s, axis=1, keepdims=True))
                p = jnp.exp2(s - m_new)
                alpha = jnp.exp2(m_i - m_new)
                new.append((m_new, alpha * l_i + jnp.sum(p, axis=1, keepdims=True),
                            alpha * acc + jnp.dot(p.astype(BF16), v_ref[ks, cols], preferred_element_type=F32)))
            return tuple(new)

        init = tuple((jnp.full((rows, 1), NEG_BIG, F32), jnp.zeros((rows, 1), F32),
                      jnp.zeros((rows, ATTN_HEAD_DIM), F32)) for _ in range(n_chain))
        res = tile_pairs(att_step, init)
        for c in range(n_chain):
            out = res[c][2] / res[c][1]
            for j in range(ATTN_QPK):
                hq = (g0 + c) * ATTN_QPK + j
                o_ref[:, hq * ATTN_HEAD_DIM:(hq + 1) * ATTN_HEAD_DIM] = out[j * qb:(j + 1) * qb].astype(o_ref.dtype)


def _dsa(qkvi, kidx2, widx, batch):
    m = qkvi.shape[0]
    seq = m // batch
    nb = seq // Q_BLOCK
    top_k = min(TOPK_MAX, seq // 4)
    c_k, c_v, c_qi = ATTN_DIM // KV_DIM, ATTN_DIM // KV_DIM + 1, (ATTN_DIM + 2 * KV_DIM) // IDX_DIM
    return pl.pallas_call(
        functools.partial(_dsa_kernel, top_k=top_k, seq=seq),
        grid=(batch, nb),
        in_specs=[pl.BlockSpec((Q_BLOCK, ATTN_DIM), lambda b, i: (b * nb + i, 0)),
                  pl.BlockSpec((Q_BLOCK, IDX_DIM), lambda b, i: (b * nb + i, c_qi)),
                  pl.BlockSpec((Q_BLOCK, LANES), lambda b, i: (b * nb + i, 0)),
                  pl.BlockSpec((seq, KV_DIM), lambda b, i: (b, c_k)),
                  pl.BlockSpec((seq, KV_DIM), lambda b, i: (b, c_v)),
                  pl.BlockSpec((seq, LANES), lambda b, i: (b, 0))],
        out_specs=pl.BlockSpec((Q_BLOCK, ATTN_DIM), lambda b, i: (b * nb + i, 0)),
        out_shape=jax.ShapeDtypeStruct((m, ATTN_DIM), BF16),
        scratch_shapes=[pltpu.VMEM((seq // KEY_TILE, KEY_TILE, Q_BLOCK), jnp.int32),
                        pltpu.VMEM((seq // KEY_TILE, Q_BLOCK, KEY_TILE), F32),
                        pltpu.VMEM((4, ATTN_QPK * Q_BLOCK, ATTN_HEAD_DIM), BF16),
                        pltpu.VMEM((IDX_HEADS // 2, 2 * Q_BLOCK, LANES), BF16)],
        compiler_params=_cparams("arbitrary", "arbitrary"),
        name="dsa_mixer",
    )(qkvi, qkvi, widx, qkvi, qkvi, kidx2)


def _merge_kernel(ys_ref, ya_ref, gs_ref, ga_ref, ws_ref, wa_ref, o_ref, wss_ref, was_ref):
    @pl.when(pl.program_id(1) == 0)
    def _():
        _cast_weight(ws_ref, wss_ref)
        _cast_weight(wa_ref, was_ref)

    ps = jnp.dot(ys_ref[...], wss_ref[...], preferred_element_type=F32)
    pa = jnp.dot(ya_ref[...], was_ref[...], preferred_element_type=F32)
    o_ref[...] = (gs_ref[...].astype(F32) * ps + ga_ref[...].astype(F32) * pa).astype(o_ref.dtype)


def _merge(y_ssm, y_attn, gates, w_ssm_out, w_attn_out, layer, tn=512, tm=512):
    m = y_ssm.shape[0]
    tm = min(tm, m)
    nt = D_MODEL // tn
    return pl.pallas_call(
        _merge_kernel,
        grid=(nt, m // tm),
        in_specs=[pl.BlockSpec((tm, D_INNER), lambda j, i: (i, 0)),
                  pl.BlockSpec((tm, ATTN_DIM), lambda j, i: (i, 0)),
                  pl.BlockSpec((tm, tn), lambda j, i: (i, j)),
                  pl.BlockSpec((tm, tn), lambda j, i: (i, nt + j)),
                  pl.BlockSpec((None, D_INNER, tn), lambda j, i: (layer, 0, j)),
                  pl.BlockSpec((None, ATTN_DIM, tn), lambda j, i: (layer, 0, j))],
        out_specs=pl.BlockSpec((tm, tn), lambda j, i: (i, j)),
        out_shape=jax.ShapeDtypeStruct((m, D_MODEL), BF16),
        scratch_shapes=[pltpu.VMEM((D_INNER, tn), BF16), pltpu.VMEM((ATTN_DIM, tn), BF16)],
        compiler_params=_cparams("arbitrary", "arbitrary"),
        name="branch_merge",
    )(y_ssm, y_attn, gates, gates, w_ssm_out, w_attn_out)


def _cast_kernel(w_ref, o_ref):
    o_ref[...] = w_ref[...].astype(o_ref.dtype)


def _cast_layer(w_all, layer, tr=512):
    _, k, n = w_all.shape
    return pl.pallas_call(
        _cast_kernel,
        grid=(k // tr,),
        in_specs=[pl.BlockSpec((None, tr, n), lambda i: (layer, i, 0))],
        out_specs=pl.BlockSpec((tr, n), lambda i: (i, 0)),
        out_shape=jax.ShapeDtypeStruct((k, n), BF16),
        compiler_params=_cparams("arbitrary"),
        name="weight_cast",
    )(w_all)


def _rowmm_kernel(a_ref, w_ref, x_ref, gpost_ref, gnext_ref, xo_ref, *rest, emit_next):
    ho_ref = rest[0] if emit_next else None
    y = jnp.dot(a_ref[...], w_ref[...], preferred_element_type=F32)
    x_new = x_ref[...] + _rms(y, gpost_ref[...])
    xo_ref[...] = x_new
    if emit_next:
        ho_ref[...] = _rms(x_new, gnext_ref[...]).astype(ho_ref.dtype)


def _rowmm(a, w_all, layer, x, g_post, g_next, tm=512):
    m, k = a.shape
    w16 = _cast_layer(w_all, layer)
    d = x.shape[1]
    tm = min(tm, m)
    emit_next = g_next is not None
    if g_next is None:
        g_next = g_post
    row = pl.BlockSpec((tm, d), lambda i: (i, 0))
    par = pl.BlockSpec((1, d), lambda i: (0, 0))
    out_specs = [row, row] if emit_next else [row]
    out_shape = [jax.ShapeDtypeStruct((m, d), F32)] + ([jax.ShapeDtypeStruct((m, d), BF16)] if emit_next else [])
    outs = pl.pallas_call(
        functools.partial(_rowmm_kernel, emit_next=emit_next),
        grid=(m // tm,),
        in_specs=[pl.BlockSpec((tm, k), lambda i: (i, 0)),
                  pl.BlockSpec((k, d), lambda i: (0, 0), pipeline_mode=pl.Buffered(1)),
                  row, par, par],
        out_specs=out_specs,
        out_shape=out_shape,
        compiler_params=_cparams("arbitrary"),
        name="out_proj_norm_residual",
    )(a, w16, x, g_post.reshape(1, d), g_next.reshape(1, d))
    return (outs[0], outs[1]) if emit_next else (outs[0], None)


def _ffn_up_kernel(a_ref, wg_ref, wv_ref, cwg_ref, cwv_ref, cbg_ref, cbv_ref, o_ref,
                   wgs_ref, wvs_ref, ug_ref, uv_ref, *, tiles_per_seq):
    pad = SUBLANES
    tm = a_ref.shape[0]
    i = pl.program_id(1)

    @pl.when(i == 0)
    def _():
        _cast_weight(wg_ref, wgs_ref)
        _cast_weight(wv_ref, wvs_ref)

    @pl.when(i % tiles_per_seq == 0)
    def _():
        ug_ref[0:pad, :] = jnp.zeros((pad, ug_ref.shape[1]), F32)
        uv_ref[0:pad, :] = jnp.zeros((pad, uv_ref.shape[1]), F32)

    a = a_ref[...]

    def conv(w_s, u_ref, cw_ref, cb_ref):
        u_ref[pad:pad + tm, :] = jnp.dot(a, w_s[...], preferred_element_type=F32)
        acc = cb_ref[...] + cw_ref[FFN_CONV - 1:FFN_CONV, :] * u_ref[pad:pad + tm, :]
        for s in range(1, FFN_CONV):
            acc = acc + cw_ref[FFN_CONV - 1 - s:FFN_CONV - s, :] * u_ref[pad - s:pad - s + tm, :]
        u_ref[0:pad, :] = u_ref[tm:tm + pad, :]
        return acc

    gte = conv(wgs_ref, ug_ref, cwg_ref, cbg_ref)
    val = conv(wvs_ref, uv_ref, cwv_ref, cbv_ref)
    o_ref[...] = (jax.nn.gelu(gte, approximate=True) * val).astype(o_ref.dtype)


def _ffn_up(h, w_up, conv_w, conv_b, layer, batch, tn=512, tm=1024):
    m, k = h.shape
    seq = m // batch
    tm = min(tm, seq)
    nt = D_FF // tn
    wspec = lambda shift: pl.BlockSpec((None, k, tn), lambda j, i: (layer, 0, j + shift))
    cwspec = lambda shift: pl.BlockSpec((FFN_CONV, tn), lambda j, i: (0, j + shift))
    cbspec = lambda shift: pl.BlockSpec((1, tn), lambda j, i: (0, j + shift))
    return pl.pallas_call(
        functools.partial(_ffn_up_kernel, tiles_per_seq=seq // tm),
        grid=(nt, m // tm),
        in_specs=[pl.BlockSpec((tm, k), lambda j, i: (i, 0)),
                  wspec(0), wspec(nt), cwspec(0), cwspec(nt), cbspec(0), cbspec(nt)],
        out_specs=pl.BlockSpec((tm, tn), lambda j, i: (i, j)),
        out_shape=jax.ShapeDtypeStruct((m, D_FF), BF16),
        scratch_shapes=[pltpu.VMEM((k, tn), BF16), pltpu.VMEM((k, tn), BF16),
                        pltpu.VMEM((tm + 2 * SUBLANES, tn), F32), pltpu.VMEM((tm + 2 * SUBLANES, tn), F32)],
        compiler_params=_cparams("arbitrary", "arbitrary"),
        name="ffn_up_conv_geglu",
    )(h, w_up, w_up, conv_w, conv_w, conv_b.reshape(1, 2 * D_FF), conv_b.reshape(1, 2 * D_FF))


def kernel(x, norm_mix_pre, norm_mix_post, norm_ffn_pre, norm_ffn_post, w_in, b_gate, conv_xbc_w, conv_xbc_b, dt_bias, a_log, d_skip, ssm_norm, w_ssm_out, w_attn_out, w_mix_out, w_up, conv_ffn_w, conv_ffn_b, w_down):
    batch, seq, d = x.shape
    depth = w_in.shape[0]
    xf = x.reshape(batch * seq, d)
    h = _norm(xf, norm_mix_pre[0])
    w_t = jnp.swapaxes(w_in, 1, 2)
    for l in range(depth):
        z = _proj(h, w_t, l, OFF_Z, D_INNER, BF16)
        xbc = _proj(h, w_t, l, OFF_XBC, CONV_DIM, BF16)
        qkvi = _proj(h, w_t, l, OFF_Q, ATTN_DIM + 2 * KV_DIM + IDX_DIM, BF16)
        gates = _proj(h, w_t, l, OFF_GATE, N_BRANCHES * D_MODEL, BF16, bias=b_gate[l])
        dt128, kidx2, widx = _smalls(h, w_t, l)
        y_ssm = _ssd(z, xbc, dt128, conv_xbc_w[l], conv_xbc_b[l], dt_bias[l], a_log[l], d_skip[l],
                     ssm_norm[l], batch)
        y_attn = _dsa(qkvi, kidx2, widx, batch)
        merged = _merge(y_ssm, y_attn, gates, w_ssm_out, w_attn_out, l)
        xf, h2 = _rowmm(merged, w_mix_out, l, xf, norm_mix_post[l], norm_ffn_pre[l])
        act = _ffn_up(h2, w_up, conv_ffn_w[l], conv_ffn_b[l], l, batch)
        xf, h = _rowmm(act, w_down, l, xf, norm_ffn_post[l], norm_mix_pre[l + 1] if l + 1 < depth else None)
    return xf.reshape(batch, seq, d)
```

```python
import functools

import numpy as np
import jax
import jax.numpy as jnp
from jax import lax
from jax.experimental import pallas as pl
from jax.experimental.pallas import tpu as pltpu

F32 = jnp.float32
BF16 = jnp.bfloat16

D_MODEL = 2048
EPS = 1e-6

D_INNER = 2 * D_MODEL
SSM_HEAD_DIM = 64
SSM_HEADS = D_INNER // SSM_HEAD_DIM
SSM_GROUPS = 8
SSM_HPG = SSM_HEADS // SSM_GROUPS
SSM_STATE = 128
SSM_CONV = 4
CONV_DIM = D_INNER + 2 * SSM_GROUPS * SSM_STATE
GROUP_W = SSM_HPG * SSM_HEAD_DIM
SSD_T = 128

CHUNK = 64
ATTN_HEAD_DIM = 128
ATTN_Q_HEADS = 16
ATTN_KV_HEADS = 4
ATTN_QPK = ATTN_Q_HEADS // ATTN_KV_HEADS
ATTN_DIM = ATTN_Q_HEADS * ATTN_HEAD_DIM
KV_DIM = ATTN_KV_HEADS * ATTN_HEAD_DIM
ATTN_SCALE = ATTN_HEAD_DIM ** -0.5
IDX_HEADS = 16
IDX_HEAD_DIM = 64
IDX_DIM = IDX_HEADS * IDX_HEAD_DIM
IDX_SCALE = (IDX_HEAD_DIM ** -0.5) * (IDX_HEADS ** -0.5)
TOPK_MAX = 256
Q_BLOCK = 128
KEY_TILE = 512

N_BRANCHES = 2
D_FF = 2 * D_MODEL
FFN_CONV = 3

IN_SPLITS = (D_INNER, CONV_DIM, SSM_HEADS, ATTN_DIM, KV_DIM, KV_DIM,
             IDX_DIM, IDX_HEAD_DIM, IDX_HEADS, N_BRANCHES * D_MODEL)
_OFFS = [0] + [int(o) for o in np.cumsum(IN_SPLITS)]
OFF_Z, OFF_XBC, OFF_DT, OFF_Q, OFF_K, OFF_V, OFF_QIDX, OFF_KIDX, OFF_WIDX, OFF_GATE, D_IN_PROJ = _OFFS

LANES = 128
SUBLANES = 8
VMEM_LIMIT = 56 * 2 ** 20

INT_MIN = -2 ** 31
NEG_BIG = -1e30


def _cparams(*sem):
    return pltpu.CompilerParams(dimension_semantics=sem, vmem_limit_bytes=VMEM_LIMIT)


def _rms(x, g):
    return x * lax.rsqrt(jnp.mean(x * x, axis=-1, keepdims=True) + EPS) * g


def _norm_kernel(x_ref, g_ref, o_ref):
    o_ref[...] = _rms(x_ref[...], g_ref[...]).astype(o_ref.dtype)


def _norm(x, g, tm=512):
    m, d = x.shape
    return pl.pallas_call(
        _norm_kernel,
        grid=(m // tm,),
        in_specs=[pl.BlockSpec((tm, d), lambda i: (i, 0)),
                  pl.BlockSpec((1, d), lambda i: (0, 0))],
        out_specs=pl.BlockSpec((tm, d), lambda i: (i, 0)),
        out_shape=jax.ShapeDtypeStruct((m, d), BF16),
        compiler_params=_cparams("arbitrary"),
        name="rmsnorm",
    )(x, g.reshape(1, d))


def _cast_weight(w_ref, wsc_ref, kc=256):
    def body(i, c):
        rows = pl.ds(pl.multiple_of(i * kc, kc), kc)
        wsc_ref[rows, :] = w_ref[rows, :].astype(BF16)
        return c

    lax.fori_loop(0, wsc_ref.shape[0] // kc, body, 0)


def _dot_nt(a, b):
    return lax.dot_general(a, b, (((1,), (1,)), ((), ())), preferred_element_type=F32)


def _proj_kernel(*refs, gate, tm):
    refs = list(refs)
    h_ref = refs.pop(0)
    w_ref = refs.pop(0)
    b_ref = refs.pop(0) if gate else None
    o_ref, wsc_ref = refs
    i = pl.program_id(1)

    @pl.when(i == 0)
    def _():
        wsc_ref[...] = w_ref[0].astype(BF16)

    a = h_ref[pl.ds(pl.multiple_of(i * tm, tm), tm), :]
    acc = _dot_nt(a, wsc_ref[...])
    if gate:
        acc = jax.nn.sigmoid(acc + b_ref[...])
    o_ref[...] = acc.astype(o_ref.dtype)


def _proj(h, w_t, layer, row_start, n_rows, out_dtype, bias=None, tn=512, tm=2048):
    m, k = h.shape
    tm = min(tm, m)
    nt = n_rows // tn
    assert row_start % SUBLANES == 0 and n_rows % tn == 0
    in_specs = [pl.BlockSpec((m, k), lambda j, i: (0, 0), pipeline_mode=pl.Buffered(1)),
                pl.BlockSpec((pl.Element(1), pl.Element(tn), pl.Element(k)),
                             lambda j, i: (layer, pl.multiple_of(row_start + j * tn, SUBLANES), 0))]
    args = [h, w_t]
    if bias is not None:
        in_specs.append(pl.BlockSpec((1, tn), lambda j, i: (0, j)))
        args.append(bias.reshape(1, n_rows))
    return pl.pallas_call(
        functools.partial(_proj_kernel, gate=bias is not None, tm=tm),
        grid=(nt, m // tm),
        in_specs=in_specs,
        out_specs=pl.BlockSpec((tm, tn), lambda j, i: (i, j)),
        out_shape=jax.ShapeDtypeStruct((m, n_rows), out_dtype),
        scratch_shapes=[pltpu.VMEM((tn, k), BF16)],
        compiler_params=_cparams("arbitrary", "arbitrary"),
        name="in_proj",
    )(*args)


def _silu(x):
    h = 0.5 * x
    return h + h * jnp.tanh(h)


def _smalls_kernel(a_ref, wdt_ref, wk_ref, ww_ref, dt_ref, kidx_ref, widx_ref):
    a = a_ref[...]
    k = a.shape[1]

    def pad_rows(w):
        return jnp.concatenate([w, jnp.zeros((LANES - w.shape[0], k), BF16)], axis=0)

    wk = wk_ref[...].astype(BF16)
    dt_ref[...] = _dot_nt(a, pad_rows(wdt_ref[...].astype(BF16)))
    widx_ref[...] = _dot_nt(a, pad_rows(ww_ref[...].astype(BF16)))
    kidx_ref[...] = _dot_nt(a, jnp.concatenate([wk, wk], axis=0)).astype(kidx_ref.dtype)


def _smalls(h, w_t, layer, tm=1024):
    m, k = h.shape
    tm = min(tm, m)
    assert 2 * IDX_HEAD_DIM == LANES and IDX_HEADS % 16 == 0
    wspec = lambda start, n: pl.BlockSpec((None, n, k), lambda i: (layer, start // n, 0))
    ospec = pl.BlockSpec((tm, LANES), lambda i: (i, 0))
    return pl.pallas_call(
        _smalls_kernel,
        grid=(m // tm,),
        in_specs=[pl.BlockSpec((tm, k), lambda i: (i, 0)),
                  wspec(OFF_DT, SSM_HEADS), wspec(OFF_KIDX, IDX_HEAD_DIM), wspec(OFF_WIDX, IDX_HEADS)],
        out_specs=[ospec, ospec, ospec],
        out_shape=[jax.ShapeDtypeStruct((m, LANES), F32),
                   jax.ShapeDtypeStruct((m, LANES), BF16),
                   jax.ShapeDtypeStruct((m, LANES), F32)],
        compiler_params=_cparams("arbitrary"),
        name="in_proj_narrow",
    )(h, w_t, w_t, w_t)


def _ssd_kernel(z_ref, xbc_ref, dt_ref, cw_ref, cb_ref, dtb_ref, alog_ref, dskip_ref, ng_ref,
                o_ref, xb_ref, xc_ref, st_ref, w8_ref):
    t = SSD_T
    pad = SUBLANES
    halo = 2 * SUBLANES

    @pl.when(pl.program_id(1) == 0)
    def _():
        xb_ref[...] = jnp.zeros_like(xb_ref)
        st_ref[...] = jnp.zeros_like(st_ref)
        for i in range(SSM_CONV):
            w8_ref[i] = jnp.broadcast_to(cw_ref[i:i + 1, :], (pad, CONV_DIM))
        w8_ref[SSM_CONV] = jnp.broadcast_to(cb_ref[...], (pad, CONV_DIM))

    xb_ref[halo:halo + t, :] = xbc_ref[...]
    sel_r = lax.broadcasted_iota(jnp.int32, ((SSM_CONV - 1) * t, xb_ref.shape[0]), 1)
    sel_t = lax.broadcasted_iota(jnp.int32, ((SSM_CONV - 1) * t, xb_ref.shape[0]), 0)
    shift = (sel_t >> (t.bit_length() - 1)) + 1
    selector = jnp.where(sel_r == halo + (sel_t & (t - 1)) - shift, 1.0, 0.0).astype(BF16)
    cw = 512
    for c in range(CONV_DIM // cw):
        sl = slice(c * cw, (c + 1) * cw)
        back = jnp.dot(selector, xb_ref[:, sl], preferred_element_type=F32)
        acc = w8_ref[SSM_CONV, :, sl][None] + w8_ref[SSM_CONV - 1, :, sl][None] * (
            xbc_ref[:, sl].astype(F32).reshape(t // pad, pad, cw))
        for i in range(1, SSM_CONV):
            acc = acc + w8_ref[SSM_CONV - 1 - i, :, sl][None] * (
                back[(i - 1) * t:i * t].reshape(t // pad, pad, cw))
        xc_ref[:, sl] = _silu(acc).reshape(t, cw)
    xb_ref[0:halo, :] = xb_ref[t:t + halo, :]

    lane = lax.broadcasted_iota(jnp.int32, (t, LANES), 1)
    row = lax.broadcasted_iota(jnp.int32, (t, LANES), 0)
    head_ok = lane < SSM_HEADS
    x_dt = dt_ref[...] + dtb_ref[...]
    dt = jnp.maximum(x_dt, 0.0) + jnp.log1p(jnp.exp(-jnp.abs(x_dt)))
    dt = jnp.where(head_ok, dt, 0.0)
    a_dt = dt * (-jnp.exp(alog_ref[...]))
    tril = row >= lane
    cs = jnp.dot(tril.astype(F32), a_dt, preferred_element_type=F32,
                 precision=lax.Precision.HIGHEST)
    cs_t = cs.T
    dt_t = dt.T
    cs_last = cs[t - 1:t, :]
    dte = jnp.exp(cs_last - cs)
    expand_rows = jnp.concatenate(
        [dt * dte, jnp.exp(cs), jnp.broadcast_to(jnp.exp(cs_last), (SUBLANES, LANES))], axis=0)
    hi = expand_rows.astype(BF16)
    rem = expand_rows - hi.astype(F32)
    mid = rem.astype(BF16)
    lo = (rem - mid.astype(F32)).astype(BF16)
    expand_rows3 = jnp.concatenate([hi, mid, lo], axis=1)

    gh = lax.broadcasted_iota(jnp.int32, (LANES, GROUP_W), 0)
    gc = lax.broadcasted_iota(jnp.int32, (LANES, GROUP_W), 1) >> (SSM_HEAD_DIM.bit_length() - 1)
    half = lax.broadcasted_iota(jnp.int32, (t, 2 * SSM_HEAD_DIM), 1) < SSM_HEAD_DIM

    for g in range(SSM_GROUPS):
        sl = slice(g * GROUP_W, (g + 1) * GROUP_W)
        expand = jnp.where(gh == gc + g * SSM_HPG, 1.0, 0.0).astype(BF16)
        ex = jnp.dot(expand_rows3, jnp.concatenate([expand, expand, expand], axis=0),
                     preferred_element_type=F32)
        dtdte_e, ecs_e, cd_e = ex[0:t], ex[t:2 * t], ex[2 * t:2 * t + 1]
        xs = xc_ref[:, sl]
        bm = xc_ref[:, D_INNER + g * SSM_STATE:D_INNER + (g + 1) * SSM_STATE]
        cm = xc_ref[:, D_INNER + (SSM_GROUPS + g) * SSM_STATE:D_INNER + (SSM_GROUPS + g + 1) * SSM_STATE]
        bm16, cm16 = bm.astype(BF16), cm.astype(BF16)
        xs16 = xs.astype(BF16)
        cb = _dot_nt(cm16, bm16)
        h_in = st_ref[:, sl]
        y = jnp.dot(cm16, h_in.astype(BF16), preferred_element_type=F32) * ecs_e
        new_states = jnp.dot(bm.T.astype(BF16), (xs * dtdte_e).astype(BF16), preferred_element_type=F32)
        st_ref[:, sl] = h_in * cd_e + new_states
        yd = []
        for j in range(SSM_HPG // 2):
            h1 = g * SSM_HPG + 2 * j
            ms = []
            for hh in (h1, h1 + 1):
                dec = jnp.exp(jnp.where(tril, cs[:, hh:hh + 1] - cs_t[hh:hh + 1, :], -jnp.inf))
                ms.append((cb * (dec * dt_t[hh:hh + 1, :])).astype(BF16))
            xp = xs16[:, 2 * j * SSM_HEAD_DIM:(2 * j + 2) * SSM_HEAD_DIM]
            zero = jnp.zeros_like(xp)
            blockdiag = jnp.concatenate([jnp.where(half, xp, zero), jnp.where(half, zero, xp)], axis=0)
            yd.append(jnp.dot(jnp.concatenate(ms, axis=1), blockdiag, preferred_element_type=F32))
        y = y + jnp.concatenate(yd, axis=1) + dskip_ref[:, sl] * xs
        y = y * _silu(z_ref[:, sl].astype(F32))
        y = y * lax.rsqrt(jnp.mean(y * y, axis=-1, keepdims=True) + EPS)
        o_ref[:, sl] = (y * ng_ref[:, sl]).astype(o_ref.dtype)


def _ssd(z, xbc, dt128, conv_w, conv_b, dt_bias, a_log, d_skip, norm_g, batch):
    m = z.shape[0]
    assert xbc.dtype == BF16
    nblk = m // batch // SSD_T
    pad_heads = lambda v: jnp.pad(v, (0, LANES - SSM_HEADS)).reshape(1, LANES)
    row_spec = lambda w: pl.BlockSpec((SSD_T, w), lambda b, i: (b * nblk + i, 0))
    par_spec = lambda r, w: pl.BlockSpec((r, w), lambda b, i: (0, 0))
    return pl.pallas_call(
        _ssd_kernel,
        grid=(batch, nblk),
        in_specs=[row_spec(D_INNER), row_spec(CONV_DIM), row_spec(LANES),
                  par_spec(SSM_CONV, CONV_DIM), par_spec(1, CONV_DIM),
                  par_spec(1, LANES), par_spec(1, LANES), par_spec(1, D_INNER), par_spec(1, D_INNER)],
        out_specs=row_spec(D_INNER),
        out_shape=jax.ShapeDtypeStruct((m, D_INNER), BF16),
        scratch_shapes=[pltpu.VMEM((2 * SSD_T, CONV_DIM), BF16),
                        pltpu.VMEM((SSD_T, CONV_DIM), F32),
                        pltpu.VMEM((SSM_STATE, D_INNER), F32),
                        pltpu.VMEM((SSM_CONV + 1, SUBLANES, CONV_DIM), F32)],
        compiler_params=_cparams("arbitrary", "arbitrary"),
        name="ssd_mixer",
    )(z, xbc, dt128, conv_w, conv_b.reshape(1, CONV_DIM), pad_heads(dt_bias), pad_heads(a_log),
      jnp.repeat(d_skip, SSM_HEAD_DIM).reshape(1, D_INNER), norm_g.reshape(1, D_INNER))


def _dsa_kernel(q_ref, qi_ref, w_ref, k_ref, v_ref, ki_ref, o_ref,
                keys_ref, bias_ref, qs_ref, qh_ref, *, top_k, seq):
    qb, kt = Q_BLOCK, KEY_TILE
    blk = pl.program_id(1)
    ntiles = (blk * qb + qb + kt - 1) // kt
    lane_q = lax.broadcasted_iota(jnp.int32, (1, qb), 1)
    limit = (((blk * qb + lane_q) >> (CHUNK.bit_length() - 1)) + 1) * CHUNK
    pos_t = lax.broadcasted_iota(jnp.int32, (kt, qb), 0)
    fold_rows = 64

    def tile_pairs(step, init):
        carry = lax.fori_loop(0, ntiles // 2, lambda tp, c: step(2 * tp, 2, c), init)
        return lax.cond(ntiles % 2 == 1, lambda c: step(ntiles - 1, 1, c), lambda c: c, carry)

    def tile_quads(step, init):
        carry = lax.fori_loop(0, ntiles // 4, lambda tq, c: step(4 * tq, 4, c), init)
        carry = lax.cond((ntiles & 2) != 0, lambda c: step(ntiles & ~3, 2, c), lambda c: c, carry)
        return lax.cond((ntiles & 1) != 0, lambda c: step(ntiles - 1, 1, c), lambda c: c, carry)

    def count(pred):
        def body(t, acc):
            x = jnp.where(pred(keys_ref[t], t * kt + pos_t), 1.0, 0.0)
            return acc + jnp.sum(x.reshape(kt // fold_rows, fold_rows, qb), axis=0)
        acc = lax.fori_loop(0, ntiles, body, jnp.zeros((fold_rows, qb), F32))
        return jnp.sum(acc, axis=0, keepdims=True)

    w_t = w_ref[...].T
    lane_d = lax.broadcasted_iota(jnp.int32, (qb, LANES), 1)
    for p in range(IDX_HEADS // 2):
        qp = qi_ref[:, p * LANES:(p + 1) * LANES]
        zero = jnp.zeros_like(qp)
        qh_ref[p, 0:qb, :] = jnp.where(lane_d < IDX_HEAD_DIM, qp, zero)
        qh_ref[p, qb:2 * qb, :] = jnp.where(lane_d < IDX_HEAD_DIM, zero, qp)

    def score_step(t, span, c):
        n = span * kt
        kx = ki_ref[pl.ds(pl.multiple_of(t * kt, kt), n), :]
        acc = jnp.zeros((n, qb), F32)
        for p in range(IDX_HEADS // 2):
            s2 = _dot_nt(kx, qh_ref[p])
            for e in range(2):
                hd = 2 * p + e
                acc = acc + jnp.maximum(s2[:, e * qb:(e + 1) * qb], 0.0) * w_t[hd:hd + 1, :]
        score = acc * IDX_SCALE + 0.0
        bits = pltpu.bitcast(score, jnp.int32)
        key = jnp.where(bits < 0, bits ^ jnp.int32(0x7FFFFFFF), bits)
        pos = t * kt + lax.broadcasted_iota(jnp.int32, (n, qb), 0)
        keys_ref[pl.ds(t, span)] = jnp.where(pos < limit, key, jnp.int32(INT_MIN)).reshape(span, kt, qb)
        return c

    tile_pairs(score_step, 0)

    def bit_step(it, carry):
        tb, cnt_ge = carry
        cand = tb | lax.shift_left(jnp.int32(1), jnp.int32(31) - it)
        cand_s = cand ^ jnp.int32(INT_MIN)
        cnt = count(lambda key, pos: key >= cand_s)
        keep = cnt >= float(top_k)
        return jnp.where(keep, cand, tb), jnp.where(keep, cnt, cnt_ge)

    every = jnp.full((1, qb), ntiles * kt, jnp.int32).astype(F32)
    tb, cnt_ge = lax.fori_loop(0, 32, bit_step, (jnp.zeros((1, qb), jnp.int32), every))
    thr = tb ^ jnp.int32(INT_MIN)

    excess = (cnt_ge > float(top_k)) & (thr != jnp.int32(INT_MIN))

    def tie_cut():
        need = float(top_k) - count(lambda key, pos: key > thr)

        def step(it, mp):
            cand = mp | lax.shift_left(jnp.int32(1), jnp.int32(seq.bit_length() - 1) - it)
            cnt = count(lambda key, pos: (key == thr) & (pos < cand))
            return jnp.where(cnt < need, cand, mp)
        return lax.fori_loop(0, seq.bit_length(), step, jnp.zeros((1, qb), jnp.int32))

    any_excess = jnp.max(jnp.where(excess, 1.0, 0.0)) > 0.0
    last_tie = lax.cond(any_excess, tie_cut, lambda: jnp.full((1, qb), seq, jnp.int32))

    def bias_tile(t, c):
        key = keys_ref[t]
        pos = t * kt + pos_t
        sel = (key > thr) | ((key == thr) & (pos <= last_tie))
        b_t = jnp.where(sel & (pos < limit), 0.0, NEG_BIG)
        for c4 in range(kt // LANES):
            bias_ref[t, :, c4 * LANES:(c4 + 1) * LANES] = b_t[c4 * LANES:(c4 + 1) * LANES, :].T
        return c

    lax.fori_loop(0, ntiles, bias_tile, 0)

    log2_scale = ATTN_SCALE * float(np.log2(np.e))
    rows = ATTN_QPK * qb
    n_chain = 4
    for g0 in range(0, ATTN_KV_HEADS, n_chain):
        for c in range(n_chain):
            for j in range(ATTN_QPK):
                hq = (g0 + c) * ATTN_QPK + j
                qs_ref[c, j * qb:(j + 1) * qb, :] = q_ref[:, hq * ATTN_HEAD_DIM:(hq + 1) * ATTN_HEAD_DIM]

        def att_step(t, span, carry):
            width = span * kt
            ks = pl.ds(pl.multiple_of(t * kt, kt), width)
            bias = jnp.concatenate([bias_ref[t + u] for u in range(span)], axis=1)[None]
            new = []
            for c in range(n_chain):
                m_i, l_i, acc = carry[c]
                cols = slice((g0 + c) * ATTN_HEAD_DIM, (g0 + c + 1) * ATTN_HEAD_DIM)
                s = _dot_nt(qs_ref[c], k_ref[ks, cols])
                s = ((s * log2_scale).reshape(ATTN_QPK, qb, width) + bias).reshape(rows, width)
                m_new = jnp.maximum(m_i, jnp.max(s, axis=1, keepdims=True))
                p = jnp.exp2(s - m_new)
                alpha = jnp.exp2(m_i - m_new)
                new.append((m_new, alpha * l_i + jnp.sum(p, axis=1, keepdims=True),
                            alpha * acc + jnp.dot(p.astype(BF16), v_ref[ks, cols], preferred_element_type=F32)))
            return tuple(new)

        init = tuple((jnp.full((rows, 1), NEG_BIG, F32), jnp.zeros((rows, 1), F32),
                      jnp.zeros((rows, ATTN_HEAD_DIM), F32)) for _ in range(n_chain))
        res = tile_quads(att_step, init)
        for c in range(n_chain):
            out = res[c][2] / res[c][1]
            for j in range(ATTN_QPK):
                hq = (g0 + c) * ATTN_QPK + j
                o_ref[:, hq * ATTN_HEAD_DIM:(hq + 1) * ATTN_HEAD_DIM] = out[j * qb:(j + 1) * qb].astype(o_ref.dtype)


def _dsa(qkvi, kidx2, widx, batch):
    m = qkvi.shape[0]
    seq = m // batch
    nb = seq // Q_BLOCK
    top_k = min(TOPK_MAX, seq // 4)
    c_k, c_v, c_qi = ATTN_DIM // KV_DIM, ATTN_DIM // KV_DIM + 1, (ATTN_DIM + 2 * KV_DIM) // IDX_DIM
    return pl.pallas_call(
        functools.partial(_dsa_kernel, top_k=top_k, seq=seq),
        grid=(batch, nb),
        in_specs=[pl.BlockSpec((Q_BLOCK, ATTN_DIM), lambda b, i: (b * nb + i, 0)),
                  pl.BlockSpec((Q_BLOCK, IDX_DIM), lambda b, i: (b * nb + i, c_qi)),
                  pl.BlockSpec((Q_BLOCK, LANES), lambda b, i: (b * nb + i, 0)),
                  pl.BlockSpec((seq, KV_DIM), lambda b, i: (b, c_k)),
                  pl.BlockSpec((seq, KV_DIM), lambda b, i: (b, c_v)),
                  pl.BlockSpec((seq, LANES), lambda b, i: (b, 0))],
        out_specs=pl.BlockSpec((Q_BLOCK, ATTN_DIM), lambda b, i: (b * nb + i, 0)),
        out_shape=jax.ShapeDtypeStruct((m, ATTN_DIM), BF16),
        scratch_shapes=[pltpu.VMEM((seq // KEY_TILE, KEY_TILE, Q_BLOCK), jnp.int32),
                        pltpu.VMEM((seq // KEY_TILE, Q_BLOCK, KEY_TILE), F32),
                        pltpu.VMEM((4, ATTN_QPK * Q_BLOCK, ATTN_HEAD_DIM), BF16),
                        pltpu.VMEM((IDX_HEADS // 2, 2 * Q_BLOCK, LANES), BF16)],
        compiler_params=_cparams("arbitrary", "arbitrary"),
        name="dsa_mixer",
    )(qkvi, qkvi, widx, qkvi, qkvi, kidx2)


def _merge_kernel(ys_ref, ya_ref, gs_ref, ga_ref, ws_ref, wa_ref, o_ref, wss_ref, was_ref):
    @pl.when(pl.program_id(1) == 0)
    def _():
        _cast_weight(ws_ref, wss_ref)
        _cast_weight(wa_ref, was_ref)

    ps = jnp.dot(ys_ref[...], wss_ref[...], preferred_element_type=F32)
    pa = jnp.dot(ya_ref[...], was_ref[...], preferred_element_type=F32)
    o_ref[...] = (gs_ref[...].astype(F32) * ps + ga_ref[...].astype(F32) * pa).astype(o_ref.dtype)


def _merge(y_ssm, y_attn, gates, w_ssm_out, w_attn_out, layer, tn=512, tm=512):
    m = y_ssm.shape[0]
    tm = min(tm, m)
    nt = D_MODEL // tn
    return pl.pallas_call(
        _merge_kernel,
        grid=(nt, m // tm),
        in_specs=[pl.BlockSpec((tm, D_INNER), lambda j, i: (i, 0)),
                  pl.BlockSpec((tm, ATTN_DIM), lambda j, i: (i, 0)),
                  pl.BlockSpec((tm, tn), lambda j, i: (i, j)),
                  pl.BlockSpec((tm, tn), lambda j, i: (i, nt + j)),
                  pl.BlockSpec((None, D_INNER, tn), lambda j, i: (layer, 0, j)),
                  pl.BlockSpec((None, ATTN_DIM, tn), lambda j, i: (layer, 0, j))],
        out_specs=pl.BlockSpec((tm, tn), lambda j, i: (i, j)),
        out_shape=jax.ShapeDtypeStruct((m, D_MODEL), BF16),
        scratch_shapes=[pltpu.VMEM((D_INNER, tn), BF16), pltpu.VMEM((ATTN_DIM, tn), BF16)],
        compiler_params=_cparams("arbitrary", "arbitrary"),
        name="branch_merge",
    )(y_ssm, y_attn, gates, gates, w_ssm_out, w_attn_out)


def _cast_kernel(w_ref, o_ref):
    o_ref[...] = w_ref[...].astype(o_ref.dtype)


def _cast_layer(w_all, layer, tr=512):
    _, k, n = w_all.shape
    return pl.pallas_call(
        _cast_kernel,
        grid=(k // tr,),
        in_specs=[pl.BlockSpec((None, tr, n), lambda i: (layer, i, 0))],
        out_specs=pl.BlockSpec((tr, n), lambda i: (i, 0)),
        out_shape=jax.ShapeDtypeStruct((k, n), BF16),
        compiler_params=_cparams("arbitrary"),
        name="weight_cast",
    )(w_all)


def _rowmm_kernel(a_ref, w_ref, x_ref, gpost_ref, gnext_ref, xo_ref, *rest, emit_next):
    ho_ref = rest[0] if emit_next else None
    y = jnp.dot(a_ref[...], w_ref[...], preferred_element_type=F32)
    x_new = x_ref[...] + _rms(y, gpost_ref[...])
    xo_ref[...] = x_new
    if emit_next:
        ho_ref[...] = _rms(x_new, gnext_ref[...]).astype(ho_ref.dtype)


def _rowmm(a, w_all, layer, x, g_post, g_next, tm=512):
    m, k = a.shape
    w16 = _cast_layer(w_all, layer)
    d = x.shape[1]
    tm = min(tm, m)
    emit_next = g_next is not None
    if g_next is None:
        g_next = g_post
    row = pl.BlockSpec((tm, d), lambda i: (i, 0))
    par = pl.BlockSpec((1, d), lambda i: (0, 0))
    out_specs = [row, row] if emit_next else [row]
    out_shape = [jax.ShapeDtypeStruct((m, d), F32)] + ([jax.ShapeDtypeStruct((m, d), BF16)] if emit_next else [])
    outs = pl.pallas_call(
        functools.partial(_rowmm_kernel, emit_next=emit_next),
        grid=(m // tm,),
        in_specs=[pl.BlockSpec((tm, k), lambda i: (i, 0)),
                  pl.BlockSpec((k, d), lambda i: (0, 0), pipeline_mode=pl.Buffered(1)),
                  row, par, par],
        out_specs=out_specs,
        out_shape=out_shape,
        compiler_params=_cparams("arbitrary"),
        name="out_proj_norm_residual",
    )(a, w16, x, g_post.reshape(1, d), g_next.reshape(1, d))
    return (outs[0], outs[1]) if emit_next else (outs[0], None)


def _ffn_up_kernel(a_ref, wg_ref, wv_ref, cwg_ref, cwv_ref, cbg_ref, cbv_ref, o_ref,
                   wgs_ref, wvs_ref, ug_ref, uv_ref, *, tiles_per_seq):
    pad = SUBLANES
    tm = a_ref.shape[0]
    i = pl.program_id(1)

    @pl.when(i == 0)
    def _():
        _cast_weight(wg_ref, wgs_ref)
        _cast_weight(wv_ref, wvs_ref)

    @pl.when(i % tiles_per_seq == 0)
    def _():
        ug_ref[0:pad, :] = jnp.zeros((pad, ug_ref.shape[1]), F32)
        uv_ref[0:pad, :] = jnp.zeros((pad, uv_ref.shape[1]), F32)

    a = a_ref[...]

    def conv(w_s, u_ref, cw_ref, cb_ref):
        u_ref[pad:pad + tm, :] = jnp.dot(a, w_s[...], preferred_element_type=F32)
        acc = cb_ref[...] + cw_ref[FFN_CONV - 1:FFN_CONV, :] * u_ref[pad:pad + tm, :]
        for s in range(1, FFN_CONV):
            acc = acc + cw_ref[FFN_CONV - 1 - s:FFN_CONV - s, :] * u_ref[pad - s:pad - s + tm, :]
        u_ref[0:pad, :] = u_ref[tm:tm + pad, :]
        return acc

    gte = conv(wgs_ref, ug_ref, cwg_ref, cbg_ref)
    val = conv(wvs_ref, uv_ref, cwv_ref, cbv_ref)
    o_ref[...] = (jax.nn.gelu(gte, approximate=True) * val).astype(o_ref.dtype)


def _ffn_up(h, w_up, conv_w, conv_b, layer, batch, tn=512, tm=1024):
    m, k = h.shape
    seq = m // batch
    tm = min(tm, seq)
    nt = D_FF // tn
    wspec = lambda shift: pl.BlockSpec((None, k, tn), lambda j, i: (layer, 0, j + shift))
    cwspec = lambda shift: pl.BlockSpec((FFN_CONV, tn), lambda j, i: (0, j + shift))
    cbspec = lambda shift: pl.BlockSpec((1, tn), lambda j, i: (0, j + shift))
    return pl.pallas_call(
        functools.partial(_ffn_up_kernel, tiles_per_seq=seq // tm),
        grid=(nt, m // tm),
        in_specs=[pl.BlockSpec((tm, k), lambda j, i: (i, 0)),
                  wspec(0), wspec(nt), cwspec(0), cwspec(nt), cbspec(0), cbspec(nt)],
        out_specs=pl.BlockSpec((tm, tn), lambda j, i: (i, j)),
        out_shape=jax.ShapeDtypeStruct((m, D_FF), BF16),
        scratch_shapes=[pltpu.VMEM((k, tn), BF16), pltpu.VMEM((k, tn), BF16),
                        pltpu.VMEM((tm + 2 * SUBLANES, tn), F32), pltpu.VMEM((tm + 2 * SUBLANES, tn), F32)],
        compiler_params=_cparams("arbitrary", "arbitrary"),
        name="ffn_up_conv_geglu",
    )(h, w_up, w_up, conv_w, conv_w, conv_b.reshape(1, 2 * D_FF), conv_b.reshape(1, 2 * D_FF))


def kernel(x, norm_mix_pre, norm_mix_post, norm_ffn_pre, norm_ffn_post, w_in, b_gate, conv_xbc_w, conv_xbc_b, dt_bias, a_log, d_skip, ssm_norm, w_ssm_out, w_attn_out, w_mix_out, w_up, conv_ffn_w, conv_ffn_b, w_down):
    batch, seq, d = x.shape
    depth = w_in.shape[0]
    xf = x.reshape(batch * seq, d)
    h = _norm(xf, norm_mix_pre[0])
    w_t = jnp.swapaxes(w_in, 1, 2)
    for l in range(depth):
        z = _proj(h, w_t, l, OFF_Z, D_INNER, BF16)
        xbc = _proj(h, w_t, l, OFF_XBC, CONV_DIM, BF16)
        qkvi = _proj(h, w_t, l, OFF_Q, ATTN_DIM + 2 * KV_DIM + IDX_DIM, BF16)
        gates = _proj(h, w_t, l, OFF_GATE, N_BRANCHES * D_MODEL, BF16, bias=b_gate[l])
        dt128, kidx2, widx = _smalls(h, w_t, l)
        y_ssm = _ssd(z, xbc, dt128, conv_xbc_w[l], conv_xbc_b[l], dt_bias[l], a_log[l], d_skip[l],
                     ssm_norm[l], batch)
        y_attn = _dsa(qkvi, kidx2, widx, batch)
        merged = _merge(y_ssm, y_attn, gates, w_ssm_out, w_attn_out, l)
        xf, h2 = _rowmm(merged, w_mix_out, l, xf, norm_mix_post[l], norm_ffn_pre[l])
        act = _ffn_up(h2, w_up, conv_ffn_w[l], conv_ffn_b[l], l, batch)
        xf, h = _rowmm(act, w_down, l, xf, norm_ffn_post[l], norm_mix_pre[l + 1] if l + 1 < depth else None)
    return xf.reshape(batch, seq, d)
```

```python
import functools

import numpy as np
import jax
import jax.numpy as jnp
from jax import lax
from jax.experimental import pallas as pl
from jax.experimental.pallas import tpu as pltpu

F32 = jnp.float32
BF16 = jnp.bfloat16

D_MODEL = 2048
EPS = 1e-6

D_INNER = 2 * D_MODEL
SSM_HEAD_DIM = 64
SSM_HEADS = D_INNER // SSM_HEAD_DIM
SSM_GROUPS = 8
SSM_HPG = SSM_HEADS // SSM_GROUPS
SSM_STATE = 128
SSM_CONV = 4
CONV_DIM = D_INNER + 2 * SSM_GROUPS * SSM_STATE
GROUP_W = SSM_HPG * SSM_HEAD_DIM
SSD_T = 128

CHUNK = 64
ATTN_HEAD_DIM = 128
ATTN_Q_HEADS = 16
ATTN_KV_HEADS = 4
ATTN_QPK = ATTN_Q_HEADS // ATTN_KV_HEADS
ATTN_DIM = ATTN_Q_HEADS * ATTN_HEAD_DIM
KV_DIM = ATTN_KV_HEADS * ATTN_HEAD_DIM
ATTN_SCALE = ATTN_HEAD_DIM ** -0.5
IDX_HEADS = 16
IDX_HEAD_DIM = 64
IDX_DIM = IDX_HEADS * IDX_HEAD_DIM
IDX_SCALE = (IDX_HEAD_DIM ** -0.5) * (IDX_HEADS ** -0.5)
TOPK_MAX = 256
Q_BLOCK = 128
KEY_TILE = 512

N_BRANCHES = 2
D_FF = 2 * D_MODEL
FFN_CONV = 3

IN_SPLITS = (D_INNER, CONV_DIM, SSM_HEADS, ATTN_DIM, KV_DIM, KV_DIM,
             IDX_DIM, IDX_HEAD_DIM, IDX_HEADS, N_BRANCHES * D_MODEL)
_OFFS = [0] + [int(o) for o in np.cumsum(IN_SPLITS)]
OFF_Z, OFF_XBC, OFF_DT, OFF_Q, OFF_K, OFF_V, OFF_QIDX, OFF_KIDX, OFF_WIDX, OFF_GATE, D_IN_PROJ = _OFFS

LANES = 128
SUBLANES = 8
VMEM_LIMIT = 56 * 2 ** 20

INT_MIN = -2 ** 31
NEG_BIG = -1e30


def _cparams(*sem):
    return pltpu.CompilerParams(dimension_semantics=sem, vmem_limit_bytes=VMEM_LIMIT)


def _rms(x, g):
    return x * lax.rsqrt(jnp.mean(x * x, axis=-1, keepdims=True) + EPS) * g


def _norm_kernel(x_ref, g_ref, o_ref):
    o_ref[...] = _rms(x_ref[...], g_ref[...]).astype(o_ref.dtype)


def _norm(x, g, tm=512):
    m, d = x.shape
    return pl.pallas_call(
        _norm_kernel,
        grid=(m // tm,),
        in_specs=[pl.BlockSpec((tm, d), lambda i: (i, 0)),
                  pl.BlockSpec((1, d), lambda i: (0, 0))],
        out_specs=pl.BlockSpec((tm, d), lambda i: (i, 0)),
        out_shape=jax.ShapeDtypeStruct((m, d), BF16),
        compiler_params=_cparams("arbitrary"),
        name="rmsnorm",
    )(x, g.reshape(1, d))


def _cast_weight(w_ref, wsc_ref, kc=256):
    def body(i, c):
        rows = pl.ds(pl.multiple_of(i * kc, kc), kc)
        wsc_ref[rows, :] = w_ref[rows, :].astype(BF16)
        return c

    lax.fori_loop(0, wsc_ref.shape[0] // kc, body, 0)


def _dot_nt(a, b):
    return lax.dot_general(a, b, (((1,), (1,)), ((), ())), preferred_element_type=F32)


def _proj_kernel(*refs, gate, tm):
    refs = list(refs)
    h_ref = refs.pop(0)
    w_ref = refs.pop(0)
    b_ref = refs.pop(0) if gate else None
    o_ref, wsc_ref = refs
    i = pl.program_id(1)

    @pl.when(i == 0)
    def _():
        wsc_ref[...] = w_ref[0].astype(BF16)

    a = h_ref[pl.ds(pl.multiple_of(i * tm, tm), tm), :]
    acc = _dot_nt(a, wsc_ref[...])
    if gate:
        acc = jax.nn.sigmoid(acc + b_ref[...])
    o_ref[...] = acc.astype(o_ref.dtype)


def _proj(h, w_t, layer, row_start, n_rows, out_dtype, bias=None, tn=512, tm=2048):
    m, k = h.shape
    tm = min(tm, m)
    nt = n_rows // tn
    assert row_start % SUBLANES == 0 and n_rows % tn == 0
    in_specs = [pl.BlockSpec((m, k), lambda j, i: (0, 0), pipeline_mode=pl.Buffered(1)),
                pl.BlockSpec((pl.Element(1), pl.Element(tn), pl.Element(k)),
                             lambda j, i: (layer, pl.multiple_of(row_start + j * tn, SUBLANES), 0))]
    args = [h, w_t]
    if bias is not None:
        in_specs.append(pl.BlockSpec((1, tn), lambda j, i: (0, j)))
        args.append(bias.reshape(1, n_rows))
    return pl.pallas_call(
        functools.partial(_proj_kernel, gate=bias is not None, tm=tm),
        grid=(nt, m // tm),
        in_specs=in_specs,
        out_specs=pl.BlockSpec((tm, tn), lambda j, i: (i, j)),
        out_shape=jax.ShapeDtypeStruct((m, n_rows), out_dtype),
        scratch_shapes=[pltpu.VMEM((tn, k), BF16)],
        compiler_params=_cparams("arbitrary", "arbitrary"),
        name="in_proj",
    )(*args)


def _silu(x):
    h = 0.5 * x
    return h + h * jnp.tanh(h)


def _smalls_kernel(a_ref, wdt_ref, wk_ref, ww_ref, dt_ref, kidx_ref, widx_ref):
    a = a_ref[...]
    k = a.shape[1]

    def pad_rows(w):
        return jnp.concatenate([w, jnp.zeros((LANES - w.shape[0], k), BF16)], axis=0)

    wk = wk_ref[...].astype(BF16)
    dt_ref[...] = _dot_nt(a, pad_rows(wdt_ref[...].astype(BF16)))
    widx_ref[...] = _dot_nt(a, pad_rows(ww_ref[...].astype(BF16)))
    kidx_ref[...] = _dot_nt(a, jnp.concatenate([wk, wk], axis=0)).astype(kidx_ref.dtype)


def _smalls(h, w_t, layer, tm=1024):
    m, k = h.shape
    tm = min(tm, m)
    assert 2 * IDX_HEAD_DIM == LANES and IDX_HEADS % 16 == 0
    wspec = lambda start, n: pl.BlockSpec((None, n, k), lambda i: (layer, start // n, 0))
    ospec = pl.BlockSpec((tm, LANES), lambda i: (i, 0))
    return pl.pallas_call(
        _smalls_kernel,
        grid=(m // tm,),
        in_specs=[pl.BlockSpec((tm, k), lambda i: (i, 0)),
                  wspec(OFF_DT, SSM_HEADS), wspec(OFF_KIDX, IDX_HEAD_DIM), wspec(OFF_WIDX, IDX_HEADS)],
        out_specs=[ospec, ospec, ospec],
        out_shape=[jax.ShapeDtypeStruct((m, LANES), F32),
                   jax.ShapeDtypeStruct((m, LANES), BF16),
                   jax.ShapeDtypeStruct((m, LANES), F32)],
        compiler_params=_cparams("arbitrary"),
        name="in_proj_narrow",
    )(h, w_t, w_t, w_t)


def _ssd_kernel(z_ref, xbc_ref, dt_ref, cw_ref, cb_ref, dtb_ref, alog_ref, dskip_ref, ng_ref,
                o_ref, xb_ref, xc_ref, st_ref, w8_ref):
    t = SSD_T
    pad = SUBLANES
    halo = 2 * SUBLANES

    @pl.when(pl.program_id(1) == 0)
    def _():
        xb_ref[...] = jnp.zeros_like(xb_ref)
        st_ref[...] = jnp.zeros_like(st_ref)
        for i in range(SSM_CONV):
            w8_ref[i] = jnp.broadcast_to(cw_ref[i:i + 1, :], (pad, CONV_DIM))
        w8_ref[SSM_CONV] = jnp.broadcast_to(cb_ref[...], (pad, CONV_DIM))

    xb_ref[halo:halo + t, :] = xbc_ref[...]
    sel_r = lax.broadcasted_iota(jnp.int32, ((SSM_CONV - 1) * t, xb_ref.shape[0]), 1)
    sel_t = lax.broadcasted_iota(jnp.int32, ((SSM_CONV - 1) * t, xb_ref.shape[0]), 0)
    shift = (sel_t >> (t.bit_length() - 1)) + 1
    selector = jnp.where(sel_r == halo + (sel_t & (t - 1)) - shift, 1.0, 0.0).astype(BF16)
    cw = 512
    for c in range(CONV_DIM // cw):
        sl = slice(c * cw, (c + 1) * cw)
        back = jnp.dot(selector, xb_ref[:, sl], preferred_element_type=F32)
        acc = w8_ref[SSM_CONV, :, sl][None] + w8_ref[SSM_CONV - 1, :, sl][None] * (
            xbc_ref[:, sl].astype(F32).reshape(t // pad, pad, cw))
        for i in range(1, SSM_CONV):
            acc = acc + w8_ref[SSM_CONV - 1 - i, :, sl][None] * (
                back[(i - 1) * t:i * t].reshape(t // pad, pad, cw))
        xc_ref[:, sl] = _silu(acc).reshape(t, cw)
    xb_ref[0:halo, :] = xb_ref[t:t + halo, :]

    lane = lax.broadcasted_iota(jnp.int32, (t, LANES), 1)
    row = lax.broadcasted_iota(jnp.int32, (t, LANES), 0)
    head_ok = lane < SSM_HEADS
    x_dt = dt_ref[...] + dtb_ref[...]
    dt = jnp.maximum(x_dt, 0.0) + jnp.log1p(jnp.exp(-jnp.abs(x_dt)))
    dt = jnp.where(head_ok, dt, 0.0)
    a_dt = dt * (-jnp.exp(alog_ref[...]))
    tril = row >= lane
    cs = jnp.dot(tril.astype(F32), a_dt, preferred_element_type=F32,
                 precision=lax.Precision.HIGHEST)
    cs_t = cs.T
    dt_t = dt.T
    cs_last = cs[t - 1:t, :]
    dte = jnp.exp(cs_last - cs)
    expand_rows = jnp.concatenate(
        [dt * dte, jnp.exp(cs), jnp.broadcast_to(jnp.exp(cs_last), (SUBLANES, LANES))], axis=0)
    hi = expand_rows.astype(BF16)
    rem = expand_rows - hi.astype(F32)
    mid = rem.astype(BF16)
    lo = (rem - mid.astype(F32)).astype(BF16)
    expand_rows3 = jnp.concatenate([hi, mid, lo], axis=1)

    gh = lax.broadcasted_iota(jnp.int32, (LANES, GROUP_W), 0)
    gc = lax.broadcasted_iota(jnp.int32, (LANES, GROUP_W), 1) >> (SSM_HEAD_DIM.bit_length() - 1)
    half = lax.broadcasted_iota(jnp.int32, (t, 2 * SSM_HEAD_DIM), 1) < SSM_HEAD_DIM

    for g in range(SSM_GROUPS):
        sl = slice(g * GROUP_W, (g + 1) * GROUP_W)
        expand = jnp.where(gh == gc + g * SSM_HPG, 1.0, 0.0).astype(BF16)
        ex = jnp.dot(expand_rows3, jnp.concatenate([expand, expand, expand], axis=0),
                     preferred_element_type=F32)
        dtdte_e, ecs_e, cd_e = ex[0:t], ex[t:2 * t], ex[2 * t:2 * t + 1]
        xs = xc_ref[:, sl]
        bm = xc_ref[:, D_INNER + g * SSM_STATE:D_INNER + (g + 1) * SSM_STATE]
        cm = xc_ref[:, D_INNER + (SSM_GROUPS + g) * SSM_STATE:D_INNER + (SSM_GROUPS + g + 1) * SSM_STATE]
        bm16, cm16 = bm.astype(BF16), cm.astype(BF16)
        xs16 = xs.astype(BF16)
        cb = _dot_nt(cm16, bm16)
        h_in = st_ref[:, sl]
        y = jnp.dot(cm16, h_in.astype(BF16), preferred_element_type=F32) * ecs_e
        new_states = jnp.dot(bm.T.astype(BF16), (xs * dtdte_e).astype(BF16), preferred_element_type=F32)
        st_ref[:, sl] = h_in * cd_e + new_states
        yd = []
        for j in range(SSM_HPG // 2):
            h1 = g * SSM_HPG + 2 * j
            ms = []
            for hh in (h1, h1 + 1):
                dec = jnp.exp(jnp.where(tril, cs[:, hh:hh + 1] - cs_t[hh:hh + 1, :], -jnp.inf))
                ms.append((cb * (dec * dt_t[hh:hh + 1, :])).astype(BF16))
            xp = xs16[:, 2 * j * SSM_HEAD_DIM:(2 * j + 2) * SSM_HEAD_DIM]
            zero = jnp.zeros_like(xp)
            blockdiag = jnp.concatenate([jnp.where(half, xp, zero), jnp.where(half, zero, xp)], axis=0)
            yd.append(jnp.dot(jnp.concatenate(ms, axis=1), blockdiag, preferred_element_type=F32))
        y = y + jnp.concatenate(yd, axis=1) + dskip_ref[:, sl] * xs
        y = y * _silu(z_ref[:, sl].astype(F32))
        y = y * lax.rsqrt(jnp.mean(y * y, axis=-1, keepdims=True) + EPS)
        o_ref[:, sl] = (y * ng_ref[:, sl]).astype(o_ref.dtype)


def _ssd(z, xbc, dt128, conv_w, conv_b, dt_bias, a_log, d_skip, norm_g, batch):
    m = z.shape[0]
    assert xbc.dtype == BF16
    nblk = m // batch // SSD_T
    pad_heads = lambda v: jnp.pad(v, (0, LANES - SSM_HEADS)).reshape(1, LANES)
    row_spec = lambda w: pl.BlockSpec((SSD_T, w), lambda b, i: (b * nblk + i, 0))
    par_spec = lambda r, w: pl.BlockSpec((r, w), lambda b, i: (0, 0))
    return pl.pallas_call(
        _ssd_kernel,
        grid=(batch, nblk),
        in_specs=[row_spec(D_INNER), row_spec(CONV_DIM), row_spec(LANES),
                  par_spec(SSM_CONV, CONV_DIM), par_spec(1, CONV_DIM),
                  par_spec(1, LANES), par_spec(1, LANES), par_spec(1, D_INNER), par_spec(1, D_INNER)],
        out_specs=row_spec(D_INNER),
        out_shape=jax.ShapeDtypeStruct((m, D_INNER), BF16),
        scratch_shapes=[pltpu.VMEM((2 * SSD_T, CONV_DIM), BF16),
                        pltpu.VMEM((SSD_T, CONV_DIM), F32),
                        pltpu.VMEM((SSM_STATE, D_INNER), F32),
                        pltpu.VMEM((SSM_CONV + 1, SUBLANES, CONV_DIM), F32)],
        compiler_params=_cparams("arbitrary", "arbitrary"),
        name="ssd_mixer",
    )(z, xbc, dt128, conv_w, conv_b.reshape(1, CONV_DIM), pad_heads(dt_bias), pad_heads(a_log),
      jnp.repeat(d_skip, SSM_HEAD_DIM).reshape(1, D_INNER), norm_g.reshape(1, D_INNER))


def _dsa_kernel(q_ref, qi_ref, w_ref, k_ref, v_ref, ki_ref, o_ref,
                keys_ref, bias_ref, qs_ref, qh_ref, *, top_k, seq):
    qb, kt = Q_BLOCK, KEY_TILE
    blk = pl.program_id(1)
    ntiles = (blk * qb + qb + kt - 1) // kt
    lane_q = lax.broadcasted_iota(jnp.int32, (1, qb), 1)
    limit = (((blk * qb + lane_q) >> (CHUNK.bit_length() - 1)) + 1) * CHUNK
    pos_t = lax.broadcasted_iota(jnp.int32, (kt, qb), 0)
    fold_rows = 64

    def tile_pairs(step, init):
        carry = lax.fori_loop(0, ntiles // 2, lambda tp, c: step(2 * tp, 2, c), init)
        return lax.cond(ntiles % 2 == 1, lambda c: step(ntiles - 1, 1, c), lambda c: c, carry)

    def count(pred):
        def body(t, acc):
            x = jnp.where(pred(keys_ref[t], t * kt + pos_t), 1.0, 0.0)
            return acc + jnp.sum(x.reshape(kt // fold_rows, fold_rows, qb), axis=0)
        acc = lax.fori_loop(0, ntiles, body, jnp.zeros((fold_rows, qb), F32))
        return jnp.sum(acc, axis=0, keepdims=True)

    w_t = w_ref[...].T
    lane_d = lax.broadcasted_iota(jnp.int32, (qb, LANES), 1)
    for p in range(IDX_HEADS // 2):
        qp = qi_ref[:, p * LANES:(p + 1) * LANES]
        zero = jnp.zeros_like(qp)
        qh_ref[p, 0:qb, :] = jnp.where(lane_d < IDX_HEAD_DIM, qp, zero)
        qh_ref[p, qb:2 * qb, :] = jnp.where(lane_d < IDX_HEAD_DIM, zero, qp)

    def score_step(t, span, c):
        n = span * kt
        kx = ki_ref[pl.ds(pl.multiple_of(t * kt, kt), n), :]
        acc = jnp.zeros((n, qb), F32)
        for p in range(IDX_HEADS // 2):
            s2 = _dot_nt(kx, qh_ref[p])
            for e in range(2):
                hd = 2 * p + e
                acc = acc + jnp.maximum(s2[:, e * qb:(e + 1) * qb], 0.0) * w_t[hd:hd + 1, :]
        score = acc * IDX_SCALE + 0.0
        bits = pltpu.bitcast(score, jnp.int32)
        key = jnp.where(bits < 0, bits ^ jnp.int32(0x7FFFFFFF), bits)
        pos = t * kt + lax.broadcasted_iota(jnp.int32, (n, qb), 0)
        keys_ref[pl.ds(t, span)] = jnp.where(pos < limit, key, jnp.int32(INT_MIN)).reshape(span, kt, qb)
        return c

    tile_pairs(score_step, 0)

    def bit_step(it, tb):
        cand = tb | lax.shift_left(jnp.int32(1), jnp.int32(31) - it)
        cand_s = cand ^ jnp.int32(INT_MIN)
        cnt = count(lambda key, pos: key >= cand_s)
        return jnp.where(cnt >= float(top_k), cand, tb)

    tb = lax.fori_loop(0, 32, bit_step, jnp.zeros((1, qb), jnp.int32))
    thr = tb ^ jnp.int32(INT_MIN)

    cnt_gt = count(lambda key, pos: key > thr)
    cnt_ge = cnt_gt + count(lambda key, pos: key == thr)
    excess = (cnt_ge > float(top_k)) & (thr != jnp.int32(INT_MIN))
    need = float(top_k) - cnt_gt

    def tie_cut():
        def step(it, mp):
            cand = mp | lax.shift_left(jnp.int32(1), jnp.int32(seq.bit_length() - 1) - it)
            cnt = count(lambda key, pos: (key == thr) & (pos < cand))
            return jnp.where(cnt < need, cand, mp)
        return lax.fori_loop(0, seq.bit_length(), step, jnp.zeros((1, qb), jnp.int32))

    any_excess = jnp.max(jnp.where(excess, 1.0, 0.0)) > 0.0
    last_tie = lax.cond(any_excess, tie_cut, lambda: jnp.full((1, qb), seq, jnp.int32))

    def bias_tile(t, c):
        key = keys_ref[t]
        pos = t * kt + pos_t
        sel = (key > thr) | ((key == thr) & (pos <= last_tie))
        bias_ref[t] = jnp.where(sel & (pos < limit), 0.0, NEG_BIG).astype(BF16)
        return c

    lax.fori_loop(0, ntiles, bias_tile, 0)

    log2_scale = ATTN_SCALE * float(np.log2(np.e))
    rows = ATTN_QPK * qb
    n_chain = 4
    for g0 in range(0, ATTN_KV_HEADS, n_chain):
        for c in range(n_chain):
            for j in range(ATTN_QPK):
                hq = (g0 + c) * ATTN_QPK + j
                qs_ref[c, j * qb:(j + 1) * qb, 0:ATTN_HEAD_DIM] = q_ref[:, hq * ATTN_HEAD_DIM:(hq + 1) * ATTN_HEAD_DIM]
                qs_ref[c, j * qb:(j + 1) * qb, ATTN_HEAD_DIM:2 * ATTN_HEAD_DIM] = jnp.where(
                    lax.broadcasted_iota(jnp.int32, (qb, qb), 0) == lax.broadcasted_iota(jnp.int32, (qb, qb), 1),
                    1.0, 0.0).astype(BF16)

        def att_step(t, span, carry):
            width = span * kt
            ks = pl.ds(pl.multiple_of(t * kt, kt), width)
            mask_t = bias_ref[pl.ds(t, span)].reshape(width, qb)
            new = []
            for c in range(n_chain):
                m_i, l_i, acc = carry[c]
                cols = slice((g0 + c) * ATTN_HEAD_DIM, (g0 + c + 1) * ATTN_HEAD_DIM)
                s = _dot_nt(qs_ref[c], jnp.concatenate([k_ref[ks, cols], mask_t], axis=1)) * log2_scale
                m_new = jnp.maximum(m_i, jnp.max(s, axis=1, keepdims=True))
                p = jnp.exp2(s - m_new)
                alpha = jnp.exp2(m_i - m_new)
                new.append((m_new, alpha * l_i + jnp.sum(p, axis=1, keepdims=True),
                            alpha * acc + jnp.dot(p.astype(BF16), v_ref[ks, cols], preferred_element_type=F32)))
            return tuple(new)

        init = tuple((jnp.full((rows, 1), NEG_BIG, F32), jnp.zeros((rows, 1), F32),
                      jnp.zeros((rows, ATTN_HEAD_DIM), F32)) for _ in range(n_chain))
        res = tile_pairs(att_step, init)
        for c in range(n_chain):
            out = res[c][2] / res[c][1]
            for j in range(ATTN_QPK):
                hq = (g0 + c) * ATTN_QPK + j
                o_ref[:, hq * ATTN_HEAD_DIM:(hq + 1) * ATTN_HEAD_DIM] = out[j * qb:(j + 1) * qb].astype(o_ref.dtype)


def _dsa(qkvi, kidx2, widx, batch):
    m = qkvi.shape[0]
    seq = m // batch
    nb = seq // Q_BLOCK
    top_k = min(TOPK_MAX, seq // 4)
    c_k, c_v, c_qi = ATTN_DIM // KV_DIM, ATTN_DIM // KV_DIM + 1, (ATTN_DIM + 2 * KV_DIM) // IDX_DIM
    return pl.pallas_call(
        functools.partial(_dsa_kernel, top_k=top_k, seq=seq),
        grid=(batch, nb),
        in_specs=[pl.BlockSpec((Q_BLOCK, ATTN_DIM), lambda b, i: (b * nb + i, 0)),
                  pl.BlockSpec((Q_BLOCK, IDX_DIM), lambda b, i: (b * nb + i, c_qi)),
                  pl.BlockSpec((Q_BLOCK, LANES), lambda b, i: (b * nb + i, 0)),
                  pl.BlockSpec((seq, KV_DIM), lambda b, i: (b, c_k)),
                  pl.BlockSpec((seq, KV_DIM), lambda b, i: (b, c_v)),
                  pl.BlockSpec((seq, LANES), lambda b, i: (b, 0))],
        out_specs=pl.BlockSpec((Q_BLOCK, ATTN_DIM), lambda b, i: (b * nb + i, 0)),
        out_shape=jax.ShapeDtypeStruct((m, ATTN_DIM), BF16),
        scratch_shapes=[pltpu.VMEM((seq // KEY_TILE, KEY_TILE, Q_BLOCK), jnp.int32),
                        pltpu.VMEM((seq // KEY_TILE, KEY_TILE, Q_BLOCK), BF16),
                        pltpu.VMEM((4, ATTN_QPK * Q_BLOCK, 2 * ATTN_HEAD_DIM), BF16),
                        pltpu.VMEM((IDX_HEADS // 2, 2 * Q_BLOCK, LANES), BF16)],
        compiler_params=_cparams("arbitrary", "arbitrary"),
        name="dsa_mixer",
    )(qkvi, qkvi, widx, qkvi, qkvi, kidx2)


def _merge_kernel(ys_ref, ya_ref, gs_ref, ga_ref, ws_ref, wa_ref, o_ref, wss_ref, was_ref):
    @pl.when(pl.program_id(1) == 0)
    def _():
        _cast_weight(ws_ref, wss_ref)
        _cast_weight(wa_ref, was_ref)

    ps = jnp.dot(ys_ref[...], wss_ref[...], preferred_element_type=F32)
    pa = jnp.dot(ya_ref[...], was_ref[...], preferred_element_type=F32)
    o_ref[...] = (gs_ref[...].astype(F32) * ps + ga_ref[...].astype(F32) * pa).astype(o_ref.dtype)


def _merge(y_ssm, y_attn, gates, w_ssm_out, w_attn_out, layer, tn=512, tm=512):
    m = y_ssm.shape[0]
    tm = min(tm, m)
    nt = D_MODEL // tn
    return pl.pallas_call(
        _merge_kernel,
        grid=(nt, m // tm),
        in_specs=[pl.BlockSpec((tm, D_INNER), lambda j, i: (i, 0)),
                  pl.BlockSpec((tm, ATTN_DIM), lambda j, i: (i, 0)),
                  pl.BlockSpec((tm, tn), lambda j, i: (i, j)),
                  pl.BlockSpec((tm, tn), lambda j, i: (i, nt + j)),
                  pl.BlockSpec((None, D_INNER, tn), lambda j, i: (layer, 0, j)),
                  pl.BlockSpec((None, ATTN_DIM, tn), lambda j, i: (layer, 0, j))],
        out_specs=pl.BlockSpec((tm, tn), lambda j, i: (i, j)),
        out_shape=jax.ShapeDtypeStruct((m, D_MODEL), BF16),
        scratch_shapes=[pltpu.VMEM((D_INNER, tn), BF16), pltpu.VMEM((ATTN_DIM, tn), BF16)],
        compiler_params=_cparams("arbitrary", "arbitrary"),
        name="branch_merge",
    )(y_ssm, y_attn, gates, gates, w_ssm_out, w_attn_out)


def _cast_kernel(w_ref, o_ref):
    o_ref[...] = w_ref[...].astype(o_ref.dtype)


def _cast_layer(w_all, layer, tr=512):
    _, k, n = w_all.shape
    return pl.pallas_call(
        _cast_kernel,
        grid=(k // tr,),
        in_specs=[pl.BlockSpec((None, tr, n), lambda i: (layer, i, 0))],
        out_specs=pl.BlockSpec((tr, n), lambda i: (i, 0)),
        out_shape=jax.ShapeDtypeStruct((k, n), BF16),
        compiler_params=_cparams("arbitrary"),
        name="weight_cast",
    )(w_all)


def _rowmm_kernel(a_ref, w_ref, x_ref, gpost_ref, gnext_ref, xo_ref, *rest, emit_next):
    ho_ref = rest[0] if emit_next else None
    y = jnp.dot(a_ref[...], w_ref[...], preferred_element_type=F32)
    x_new = x_ref[...] + _rms(y, gpost_ref[...])
    xo_ref[...] = x_new
    if emit_next:
        ho_ref[...] = _rms(x_new, gnext_ref[...]).astype(ho_ref.dtype)


def _rowmm(a, w_all, layer, x, g_post, g_next, tm=512):
    m, k = a.shape
    w16 = _cast_layer(w_all, layer)
    d = x.shape[1]
    tm = min(tm, m)
    emit_next = g_next is not None
    if g_next is None:
        g_next = g_post
    row = pl.BlockSpec((tm, d), lambda i: (i, 0))
    par = pl.BlockSpec((1, d), lambda i: (0, 0))
    out_specs = [row, row] if emit_next else [row]
    out_shape = [jax.ShapeDtypeStruct((m, d), F32)] + ([jax.ShapeDtypeStruct((m, d), BF16)] if emit_next else [])
    outs = pl.pallas_call(
        functools.partial(_rowmm_kernel, emit_next=emit_next),
        grid=(m // tm,),
        in_specs=[pl.BlockSpec((tm, k), lambda i: (i, 0)),
                  pl.BlockSpec((k, d), lambda i: (0, 0), pipeline_mode=pl.Buffered(1)),
                  row, par, par],
        out_specs=out_specs,
        out_shape=out_shape,
        compiler_params=_cparams("arbitrary"),
        name="out_proj_norm_residual",
    )(a, w16, x, g_post.reshape(1, d), g_next.reshape(1, d))
    return (outs[0], outs[1]) if emit_next else (outs[0], None)


def _ffn_up_kernel(a_ref, wg_ref, wv_ref, cwg_ref, cwv_ref, cbg_ref, cbv_ref, o_ref,
                   wgs_ref, wvs_ref, ug_ref, uv_ref, *, tiles_per_seq):
    pad = SUBLANES
    tm = a_ref.shape[0]
    i = pl.program_id(1)

    @pl.when(i == 0)
    def _():
        _cast_weight(wg_ref, wgs_ref)
        _cast_weight(wv_ref, wvs_ref)

    @pl.when(i % tiles_per_seq == 0)
    def _():
        ug_ref[0:pad, :] = jnp.zeros((pad, ug_ref.shape[1]), F32)
        uv_ref[0:pad, :] = jnp.zeros((pad, uv_ref.shape[1]), F32)

    a = a_ref[...]

    def conv(w_s, u_ref, cw_ref, cb_ref):
        u_ref[pad:pad + tm, :] = jnp.dot(a, w_s[...], preferred_element_type=F32)
        acc = cb_ref[...] + cw_ref[FFN_CONV - 1:FFN_CONV, :] * u_ref[pad:pad + tm, :]
        for s in range(1, FFN_CONV):
            acc = acc + cw_ref[FFN_CONV - 1 - s:FFN_CONV - s, :] * u_ref[pad - s:pad - s + tm, :]
        u_ref[0:pad, :] = u_ref[tm:tm + pad, :]
        return acc

    gte = conv(wgs_ref, ug_ref, cwg_ref, cbg_ref)
    val = conv(wvs_ref, uv_ref, cwv_ref, cbv_ref)
    o_ref[...] = (jax.nn.gelu(gte, approximate=True) * val).astype(o_ref.dtype)


def _ffn_up(h, w_up, conv_w, conv_b, layer, batch, tn=512, tm=1024):
    m, k = h.shape
    seq = m // batch
    tm = min(tm, seq)
    nt = D_FF // tn
    wspec = lambda shift: pl.BlockSpec((None, k, tn), lambda j, i: (layer, 0, j + shift))
    cwspec = lambda shift: pl.BlockSpec((FFN_CONV, tn), lambda j, i: (0, j + shift))
    cbspec = lambda shift: pl.BlockSpec((1, tn), lambda j, i: (0, j + shift))
    return pl.pallas_call(
        functools.partial(_ffn_up_kernel, tiles_per_seq=seq // tm),
        grid=(nt, m // tm),
        in_specs=[pl.BlockSpec((tm, k), lambda j, i: (i, 0)),
                  wspec(0), wspec(nt), cwspec(0), cwspec(nt), cbspec(0), cbspec(nt)],
        out_specs=pl.BlockSpec((tm, tn), lambda j, i: (i, j)),
        out_shape=jax.ShapeDtypeStruct((m, D_FF), BF16),
        scratch_shapes=[pltpu.VMEM((k, tn), BF16), pltpu.VMEM((k, tn), BF16),
                        pltpu.VMEM((tm + 2 * SUBLANES, tn), F32), pltpu.VMEM((tm + 2 * SUBLANES, tn), F32)],
        compiler_params=_cparams("arbitrary", "arbitrary"),
        name="ffn_up_conv_geglu",
    )(h, w_up, w_up, conv_w, conv_w, conv_b.reshape(1, 2 * D_FF), conv_b.reshape(1, 2 * D_FF))


def kernel(x, norm_mix_pre, norm_mix_post, norm_ffn_pre, norm_ffn_post, w_in, b_gate, conv_xbc_w, conv_xbc_b, dt_bias, a_log, d_skip, ssm_norm, w_ssm_out, w_attn_out, w_mix_out, w_up, conv_ffn_w, conv_ffn_b, w_down):
    batch, seq, d = x.shape
    depth = w_in.shape[0]
    xf = x.reshape(batch * seq, d)
    h = _norm(xf, norm_mix_pre[0])
    w_t = jnp.swapaxes(w_in, 1, 2)
    for l in range(depth):
        z = _proj(h, w_t, l, OFF_Z, D_INNER, BF16)
        xbc = _proj(h, w_t, l, OFF_XBC, CONV_DIM, BF16)
        qkvi = _proj(h, w_t, l, OFF_Q, ATTN_DIM + 2 * KV_DIM + IDX_DIM, BF16)
        gates = _proj(h, w_t, l, OFF_GATE, N_BRANCHES * D_MODEL, BF16, bias=b_gate[l])
        dt128, kidx2, widx = _smalls(h, w_t, l)
        y_ssm = _ssd(z, xbc, dt128, conv_xbc_w[l], conv_xbc_b[l], dt_bias[l], a_log[l], d_skip[l],
                     ssm_norm[l], batch)
        y_attn = _dsa(qkvi, kidx2, widx, batch)
        merged = _merge(y_ssm, y_attn, gates, w_ssm_out, w_attn_out, l)
        xf, h2 = _rowmm(merged, w_mix_out, l, xf, norm_mix_post[l], norm_ffn_pre[l])
        act = _ffn_up(h2, w_up, conv_ffn_w[l], conv_ffn_b[l], l, batch)
        xf, h = _rowmm(act, w_down, l, xf, norm_ffn_post[l], norm_mix_pre[l + 1] if l + 1 < depth else None)
    return xf.reshape(batch, seq, d)
```
